```python
import math
import jax
import jax.numpy as jnp
from jax import lax
import numpy as np

D_MODEL = 2048
BATCH = 8
SEQ = 2048
DEPTH = 2

GRID_W = 64
CTX_LEN = 256
N_MIXERS = 4
GROUP_W = D_MODEL // N_MIXERS
NA_HEADS = 8
NA_HEAD_DIM = GROUP_W // NA_HEADS
NA_WIN_H = 8
NA_WIN_W = 16
ML_HEADS = 4
ML_HEAD_DIM = GROUP_W // ML_HEADS
ML_CHUNK = 64
ML_FORGET_BIAS = 3.0
HY_CH = GROUP_W
HY_ORDER = 2
HY_SHORT = 3
HY_BANDS = 8
HY_POS_DIM = 1 + 2 * HY_BANDS
HY_HIDDEN = 64
HY_SIN_FREQ = 1.0
HY_FAST_DECAY = 0.3
HY_SLOW_DECAY = 1.5
HY_DECAY_TARGET = 1e-2
DN_HEADS = 4
DN_HEAD_DIM = GROUP_W // DN_HEADS
DN_CONV = 3
DN_CHUNK = 64
N_EXPERTS = 32
TOP_K = 4
D_FF = D_MODEL
SWIGLU_LIMIT = 7.0
SWIGLU_ALPHA = 1.702
MOE_BLOCK = 256
ROPE_BASE = 10000.0
EPS = 1e-6
IN_LAYOUT = (('na_qkv', 3 * GROUP_W), ('ml_qkv', 3 * GROUP_W), ('ml_o', GROUP_W), ('ml_gates', 4 * ML_HEADS),
             ('hy_u', 3 * HY_CH), ('dn_qkv', 3 * GROUP_W), ('dn_gate', GROUP_W), ('dn_beta', 2 * DN_HEADS),
             ('dn_a', 2 * DN_HEADS))
IN_COLS = 12 * GROUP_W + 2 * GROUP_W + 4 * ML_HEADS + 4 * DN_HEADS

kernel_name = 'hybrid_na_mlstm_hyena_deltanet_moe_dit'

F32 = jnp.float32


def _rms_norm(x, g):
    xf = x.astype(F32)
    y = xf * lax.rsqrt(jnp.mean(xf * xf, axis=-1, keepdims=True) + EPS)
    return (y * g.astype(F32)).astype(x.dtype)


def _l2_normalize(x):
    return x * lax.rsqrt(jnp.sum(x * x, axis=-1, keepdims=True) + EPS)


def _split_projection(p):
    sizes = [s for _, s in IN_LAYOUT]
    parts = jnp.split(p, np.cumsum(sizes)[:-1].tolist(), axis=-1)
    return {name: part for (name, _), part in zip(IN_LAYOUT, parts)}


def _to_heads(t, n_heads):
    B, L, W = t.shape
    return t.reshape(B, L, n_heads, W // n_heads).transpose(0, 2, 1, 3)


def _from_heads(t):
    B, H, L, d = t.shape
    return t.transpose(0, 2, 1, 3).reshape(B, L, H * d)


def _to_chunks(a, cs):
    B, H, L = a.shape[:3]
    return jnp.moveaxis(a.reshape(B, H, L // cs, cs, *a.shape[3:]), 2, 0)


def _from_chunks(y):
    nc, B, H, cs, d = y.shape
    return jnp.moveaxis(y, 0, 2).reshape(B, H, nc * cs, d)


def _centred_depthwise_conv(x, w):
    K, C = w.shape
    return lax.conv_general_dilated(x, w[:, None, :].astype(x.dtype), window_strides=(1,),
                                    padding=[(K // 2, K // 2)], dimension_numbers=('NWC', 'WIO', 'NWC'),
                                    feature_group_count=C)


def _axial_rope(x):
    L, dh = x.shape[2], x.shape[3]
    n_freq = dh // 4
    t = jnp.arange(L)
    row = (t // GRID_W).astype(F32)
    col = (t % GRID_W).astype(F32)
    inv = ROPE_BASE ** (-jnp.arange(n_freq, dtype=F32) / n_freq)
    ang = jnp.concatenate([row[:, None] * inv, col[:, None] * inv], axis=-1)
    cos, sin = jnp.cos(ang), jnp.sin(ang)
    xf = x.astype(F32)
    x1, x2 = xf[..., 0::2], xf[..., 1::2]
    out = jnp.stack([x1 * cos - x2 * sin, x1 * sin + x2 * cos], axis=-1).reshape(x.shape)
    return out.astype(x.dtype)


def _bidirectional(chunk_fn, init, ctx_f, ctx_b, lat_f, lat_b):
    flip = lambda seq: tuple(jnp.flip(a, axis=2) for a in seq)
    st_f, yc_f = chunk_fn(ctx_f, init)
    st_b, yc_b = chunk_fn(flip(ctx_b), init)
    _, yl_f = chunk_fn(lat_f, st_f)
    _, yl_b = chunk_fn(flip(lat_b), st_b)
    return yl_f + jnp.flip(yl_b, axis=2), yc_f + jnp.flip(yc_b, axis=2)


def _neighbourhood_attention(q, k, v, kc, vc, rpb):
    B, L, H, d = q.shape
    rows = L // GRID_W
    kh = min(NA_WIN_H, rows)
    scale = d ** -0.5
    qg = q.reshape(B, rows, GRID_W, H, d)
    kg = k.reshape(B, rows, GRID_W, H, d)
    vg = v.reshape(B, rows, GRID_W, H, d)
    cols = np.arange(GRID_W)
    col_idx = np.clip(cols - NA_WIN_W // 2, 0, GRID_W - NA_WIN_W)[:, None] + np.arange(NA_WIN_W)
    col_bias_idx = col_idx - cols[:, None] + NA_WIN_W - 1
    row_start = np.clip(np.arange(rows) - kh // 2, 0, rows - kh).astype(np.int32)
    n_loc = kh * NA_WIN_W

    def one_row(args):
        r, rs = args
        qr = lax.dynamic_index_in_dim(qg, r, axis=1, keepdims=False)
        kb = lax.dynamic_slice_in_dim(kg, rs, kh, axis=1)[:, :, col_idx]
        vb = lax.dynamic_slice_in_dim(vg, rs, kh, axis=1)[:, :, col_idx]
        bias = rpb[:, rs + jnp.arange(kh) - r + NA_WIN_H - 1][:, :, col_bias_idx]
        s_loc = jnp.einsum('bqhd,bwqjhd->bhqwj', qr, kb).astype(F32) * scale \
            + bias.transpose(0, 2, 1, 3).astype(F32)
        s_ctx = jnp.einsum('bqhd,bchd->bhqc', qr, kc).astype(F32) * scale
        s = jnp.concatenate([s_loc.reshape(B, H, GRID_W, n_loc), s_ctx], axis=-1)
        p = jax.nn.softmax(s, axis=-1).astype(v.dtype)
        p_loc = p[..., :n_loc].reshape(B, H, GRID_W, kh, NA_WIN_W)
        return jnp.einsum('bhqwj,bwqjhd->bqhd', p_loc, vb) + jnp.einsum('bhqc,bchd->bqhd', p[..., n_loc:], vc)

    out = lax.map(one_row, (jnp.arange(rows, dtype=jnp.int32), jnp.asarray(row_start)))
    return out.transpose(1, 0, 2, 3, 4).reshape(B, L, H * d)


def _context_attention(qc, kc, vc):
    B, Lc, H, d = qc.shape
    s = jnp.einsum('bqhd,bkhd->bhqk', qc, kc).astype(F32) * d ** -0.5
    p = jax.nn.softmax(s, axis=-1).astype(vc.dtype)
    return jnp.einsum('bhqk,bkhd->bqhd', p, vc).reshape(B, Lc, H * d)


def _na_mixer(p, pc, rpb, need_ctx):
    B, L, _ = p.shape
    Lc = pc.shape[1]
    q, k, v = [t.reshape(B, L, NA_HEADS, NA_HEAD_DIM) for t in jnp.split(p, 3, axis=-1)]
    qc, kc, vc = [t.reshape(B, Lc, NA_HEADS, NA_HEAD_DIM) for t in jnp.split(pc, 3, axis=-1)]
    y = _neighbourhood_attention(q, k, v, kc, vc, rpb)
    yc = _context_attention(qc, kc, vc) if need_ctx else None
    return y, yc


def _mlstm_chunks(seq, state):
    cs = ML_CHUNK
    tril = jnp.tril(jnp.ones((cs, cs), bool))

    def step(carry, inp):
        C, n, m = carry
        qc, kc, vc, ic, fc = inp
        b = jnp.cumsum(fc, axis=-1)
        a = b + m[..., None]
        dlog = jnp.where(tril, b[..., :, None] - b[..., None, :] + ic[..., None, :], -jnp.inf)
        mt = jnp.maximum(a, jnp.max(dlog, axis=-1))
        s = jnp.einsum('bhtd,bhsd->bhts', qc, kc) * jnp.exp(dlog - mt[..., None])
        w_inter = jnp.exp(a - mt)
        num = jnp.einsum('bhts,bhse->bhte', s, vc) + w_inter[..., None] * jnp.einsum('bhtd,bhde->bhte', qc, C)
        den = jnp.sum(s, axis=-1) + w_inter * jnp.einsum('bhtd,bhd->bht', qc, n)
        h = num / jnp.maximum(jnp.abs(den), jnp.exp(-mt))[..., None]
        b_last = b[..., -1:]
        g = b_last - b + ic
        m_new = jnp.maximum(b_last[..., 0] + m, jnp.max(g, axis=-1))
        wk = jnp.exp(g - m_new[..., None])
        decay = jnp.exp(b_last[..., 0] + m - m_new)
        C = decay[..., None, None] * C + jnp.einsum('bhs,bhsd,bhse->bhde', wk, kc, vc)
        n = decay[..., None] * n + jnp.einsum('bhs,bhsd->bhd', wk, kc)
        return (C, n, m_new), h

    state, h = lax.scan(step, state, tuple(_to_chunks(a, cs) for a in seq))
    return state, _from_chunks(h)


def _head_layer_norm(h, g):
    mu = jnp.mean(h, axis=-1, keepdims=True)
    var = jnp.mean(jnp.square(h - mu), axis=-1, keepdims=True)
    return _from_heads((h - mu) * lax.rsqrt(var + EPS)) * g.astype(F32)


def _mlstm_mixer(p, pc, gate_b, norm_g, need_ctx):
    def prep(qkv, gates, rope):
        B, L, _ = qkv.shape
        q, k, v = [_to_heads(t, ML_HEADS).astype(F32) for t in jnp.split(qkv, 3, axis=-1)]
        if rope:
            q, k = _axial_rope(q), _axial_rope(k)
        q = q * ML_HEAD_DIM ** -0.5
        g = (gates.astype(F32) + gate_b.reshape(-1).astype(F32)).reshape(B, L, 4, ML_HEADS).transpose(2, 0, 3, 1)
        fwd = (q, k, v, g[0], jax.nn.log_sigmoid(g[1]))
        bwd = (q, k, v, g[2], jax.nn.log_sigmoid(g[3]))
        return fwd, bwd

    lat_f, lat_b = prep(p['ml_qkv'], p['ml_gates'], True)
    ctx_f, ctx_b = prep(pc['ml_qkv'], pc['ml_gates'], False)
    B = p['ml_qkv'].shape[0]
    init = (jnp.zeros((B, ML_HEADS, ML_HEAD_DIM, ML_HEAD_DIM), F32),
            jnp.zeros((B, ML_HEADS, ML_HEAD_DIM), F32), jnp.zeros((B, ML_HEADS), F32))
    h, hc = _bidirectional(_mlstm_chunks, init, ctx_f, ctx_b, lat_f, lat_b)

    def out(hh, o):
        return (_head_layer_norm(hh, norm_g) * jax.nn.sigmoid(o.astype(F32))).astype(o.dtype)

    return out(h, p['ml_o']), (out(hc, pc['ml_o']) if need_ctx else None)


def _hyena_filters(L, w1, b1, w2, b2, w3, b3):
    t = jnp.arange(L, dtype=F32) / L
    ang = 2.0 * math.pi * t[:, None] * jnp.arange(1, HY_BANDS + 1, dtype=F32)
    feats = jnp.concatenate([t[:, None], jnp.sin(ang), jnp.cos(ang)], axis=-1)
    h = jnp.sin(HY_SIN_FREQ * (feats @ w1.astype(F32) + b1.astype(F32)))
    h = jnp.sin(HY_SIN_FREQ * (h @ w2.astype(F32) + b2.astype(F32)))
    h = (h @ w3.astype(F32) + b3.astype(F32)).reshape(L, HY_ORDER, 2, HY_CH)
    max_decay = math.log(HY_DECAY_TARGET) / HY_FAST_DECAY
    min_decay = math.log(HY_DECAY_TARGET) / HY_SLOW_DECAY
    deltas = jnp.abs(jnp.linspace(min_decay, max_decay, HY_CH, dtype=F32))
    window = jnp.exp(-t[:, None] * deltas)
    return h * window[:, None, None, :]


def _bidir_fft_conv(z, h_fwd, h_bwd, skip):
    B, L, C = z.shape
    g = jnp.concatenate([h_fwd, jnp.zeros((1, C), F32), h_bwd[:0:-1]], axis=0)
    zf = jnp.fft.rfft(z.astype(F32), n=2 * L, axis=1)
    gf = jnp.fft.rfft(g, n=2 * L, axis=0)
    y = jnp.fft.irfft(zf * gf[None], n=2 * L, axis=1)[:, :L]
    return y + z.astype(F32) * skip.astype(F32)


def _hyena(u, conv_w, conv_b, w1, b1, w2, b2, w3, b3, skip):
    L = u.shape[1]
    u = (_centred_depthwise_conv(u, conv_w) + conv_b.astype(u.dtype)).astype(F32)
    x1, x2, z = jnp.split(u, 3, axis=-1)
    h = _hyena_filters(L, w1, b1, w2, b2, w3, b3)
    z = x1 * _bidir_fft_conv(z, h[:, 0, 0], h[:, 0, 1], skip[0])
    return x2 * _bidir_fft_conv(z, h[:, 1, 0], h[:, 1, 1], skip[1])


def _hyena_mixer(p, pc, conv_w, conv_b, w1, b1, w2, b2, w3, b3, skip, need_ctx):
    y = _hyena(p['hy_u'], conv_w, conv_b, w1, b1, w2, b2, w3, b3, skip).astype(p['hy_u'].dtype)
    yc = _hyena(pc['hy_u'], conv_w, conv_b, w1, b1, w2, b2, w3, b3, skip).astype(pc['hy_u'].dtype) if need_ctx else None
    return y, yc


def _gated_delta_chunks(seq, S):
    cs = DN_CHUNK
    tril = jnp.tril(jnp.ones((cs, cs), bool))
    strict = jnp.tril(jnp.ones((cs, cs), bool), -1)
    eye = jnp.eye(cs, dtype=F32)

    def step(S, inp):
        qc, kc, vc, ac, bc = inp
        g = jnp.cumsum(ac, axis=-1)
        dec = jnp.exp(jnp.where(tril, g[..., :, None] - g[..., None, :], -jnp.inf))
        kb = kc * bc[..., None]
        m = jnp.where(strict, jnp.einsum('bhtd,bhsd->bhts', kb, kc) * dec, 0.0)
        rhs = jnp.concatenate([vc * bc[..., None], kb * jnp.exp(g)[..., None]], axis=-1)
        sol = lax.linalg.triangular_solve(eye + m, rhs, left_side=True, lower=True, unit_diagonal=True)
        dv = vc.shape[-1]
        u, w = sol[..., :dv], sol[..., dv:]
        v_new = u - jnp.einsum('bhtd,bhde->bhte', w, S)
        attn = jnp.einsum('bhtd,bhsd->bhts', qc, kc) * dec
        o = jnp.einsum('bhtd,bhde->bhte', qc * jnp.exp(g)[..., None], S) + jnp.einsum('bhts,bhse->bhte', attn, v_new)
        g_last = g[..., -1:]
        S = jnp.exp(g_last)[..., None] * S + jnp.einsum('bhsd,bhse->bhde', kc * jnp.exp(g_last - g)[..., None], v_new)
        return S, o

    S, o = lax.scan(step, S, tuple(_to_chunks(a, cs) for a in seq))
    return S, _from_chunks(o)


def _deltanet_mixer(p, pc, conv_w, a_log, dt_bias, norm_g, need_ctx):
    def prep(qkv, beta_raw, a_raw):
        B, L, _ = qkv.shape
        u = jax.nn.silu(_centred_depthwise_conv(qkv, conv_w))
        q, k, v = [_to_heads(t, DN_HEADS).astype(F32) for t in jnp.split(u, 3, axis=-1)]
        q = _l2_normalize(q) * DN_HEAD_DIM ** -0.5
        k = _l2_normalize(k)
        beta = jax.nn.sigmoid(beta_raw.astype(F32)).reshape(B, L, 2, DN_HEADS).transpose(2, 0, 3, 1)
        log_a = (-jnp.exp(a_log.astype(F32)) * jax.nn.softplus(
            a_raw.astype(F32).reshape(B, L, 2, DN_HEADS) + dt_bias.astype(F32))).transpose(2, 0, 3, 1)
        return (q, k, v, log_a[0], beta[0]), (q, k, v, log_a[1], beta[1])

    lat_f, lat_b = prep(p['dn_qkv'], p['dn_beta'], p['dn_a'])
    ctx_f, ctx_b = prep(pc['dn_qkv'], pc['dn_beta'], pc['dn_a'])
    B = p['dn_qkv'].shape[0]
    init = jnp.zeros((B, DN_HEADS, DN_HEAD_DIM, DN_HEAD_DIM), F32)
    o, oc = _bidirectional(_gated_delta_chunks, init, ctx_f, ctx_b, lat_f, lat_b)

    def out(oo, gate):
        oo = oo * lax.rsqrt(jnp.mean(oo * oo, axis=-1, keepdims=True) + EPS) * norm_g.astype(F32)
        return (_from_heads(oo) * jax.nn.silu(gate.astype(F32))).astype(gate.dtype)

    return out(o, p['dn_gate']), (out(oc, pc['dn_gate']) if need_ctx else None)


def _mixer_block(h, hc, w_in, w_out, na_rpb, ml_gate_b, ml_norm_g, hy_conv_w, hy_conv_b, hy_f_w1, hy_f_b1,
                 hy_f_w2, hy_f_b2, hy_f_w3, hy_f_b3, hy_skip, dn_conv_w, dn_a_log, dn_dt_bias, dn_norm_g, need_ctx):
    p = _split_projection(h @ w_in)
    pc = _split_projection(hc @ w_in)
    ya, yac = _na_mixer(p['na_qkv'], pc['na_qkv'], na_rpb, need_ctx)
    yb, ybc = _mlstm_mixer(p, pc, ml_gate_b, ml_norm_g, need_ctx)
    yh, yhc = _hyena_mixer(p, pc, hy_conv_w, hy_conv_b, hy_f_w1, hy_f_b1, hy_f_w2, hy_f_b2, hy_f_w3, hy_f_b3,
                           hy_skip, need_ctx)
    yd, ydc = _deltanet_mixer(p, pc, dn_conv_w, dn_a_log, dn_dt_bias, dn_norm_g, need_ctx)
    y = jnp.concatenate([ya, yb, yh, yd], axis=-1) @ w_out
    yc = jnp.concatenate([yac, ybc, yhc, ydc], axis=-1) @ w_out if need_ctx else None
    return y, yc


def _clamped_swiglu(h):
    gate, up = jnp.split(h, 2, axis=-1)
    gate = jnp.minimum(gate, SWIGLU_LIMIT)
    up = jnp.clip(up, -SWIGLU_LIMIT, SWIGLU_LIMIT)
    return gate * jax.nn.sigmoid(SWIGLU_ALPHA * gate) * (up + 1.0)


def _moe(xt, router_w, router_b, w_gu, b_gu, w_down, b_down):
    T, D = xt.shape
    logits = (xt @ router_w + router_b).astype(F32)
    top_val, top_idx = lax.top_k(logits, TOP_K)
    gates = jax.nn.softmax(top_val, axis=-1)
    A = T * TOP_K
    flat_e = top_idx.reshape(-1)
    flat_t = jnp.arange(A, dtype=jnp.int32) // TOP_K
    order = jnp.argsort(flat_e)
    sorted_e = flat_e[order]
    counts = jnp.bincount(flat_e, length=N_EXPERTS)
    padded = (counts + MOE_BLOCK - 1) // MOE_BLOCK * MOE_BLOCK
    pad_start = jnp.cumsum(padded) - padded
    grp_start = jnp.cumsum(counts) - counts
    dest = pad_start[sorted_e] + jnp.arange(A, dtype=jnp.int32) - grp_start[sorted_e]
    n_slots = (A // MOE_BLOCK + N_EXPERTS) * MOE_BLOCK
    slot_tok = jnp.full((n_slots,), T, jnp.int32).at[dest].set(flat_t[order])
    slot_gate = jnp.zeros((n_slots,), F32).at[dest].set(gates.reshape(-1)[order])
    x_pad = jnp.concatenate([xt, jnp.zeros((1, D), xt.dtype)], axis=0)

    def expert(y, args):
        w1, b1, w2, b2, start, n_blk = args

        def block(j, y):
            off = start + j * MOE_BLOCK
            tok = lax.dynamic_slice_in_dim(slot_tok, off, MOE_BLOCK)
            gate = lax.dynamic_slice_in_dim(slot_gate, off, MOE_BLOCK)
            hb = _clamped_swiglu(x_pad[tok] @ w1 + b1)
            out = (hb @ w2 + b2).astype(F32) * gate[:, None]
            return y.at[tok].add(out)

        return lax.fori_loop(0, n_blk, block, y), None

    y0 = jnp.zeros((T + 1, D), F32)
    y, _ = lax.scan(expert, y0, (w_gu, b_gu, w_down, b_down, pad_start, padded // MOE_BLOCK))
    return y[:T].astype(xt.dtype)


def setup_inputs(seed: int = 0) -> dict:
    key = jax.random.key(seed)
    keys = iter(jax.random.split(key, 48))
    D = D_MODEL

    def dense(shape, fan_in, gain=1.0):
        bound = gain * math.sqrt(3.0 / fan_in)
        return jax.random.uniform(next(keys), shape, F32, -bound, bound)

    def normal(shape, std):
        return std * jax.random.normal(next(keys), shape, F32)

    x = normal((BATCH, SEQ, D), 1.0)
    c = normal((BATCH, D), 1.0)
    ctx = normal((BATCH, CTX_LEN, D), 1.0)
    c_ctx = normal((D,), 1.0)
    w_ada = dense((DEPTH, D, 6 * D), D, 0.5)
    b_ada = normal((DEPTH, 6 * D), 0.02)
    norm_g = 1.0 + normal((DEPTH, 4, D), 0.1)
    w_in = dense((DEPTH, D, IN_COLS), D)
    w_out = dense((DEPTH, D, D), D)
    na_rpb = normal((DEPTH, NA_HEADS, 2 * NA_WIN_H - 1, 2 * NA_WIN_W - 1), 0.1)
    ml_gate_b = normal((DEPTH, 4, ML_HEADS), 0.3) + jnp.array([0.0, ML_FORGET_BIAS, 0.0, ML_FORGET_BIAS], F32)[None, :, None]
    ml_norm_g = 1.0 + normal((DEPTH, GROUP_W), 0.1)
    hy_conv_w = dense((DEPTH, HY_SHORT, 3 * HY_CH), HY_SHORT)
    hy_conv_b = normal((DEPTH, 3 * HY_CH), 0.02)
    hy_f_w1 = normal((DEPTH, HY_POS_DIM, HY_HIDDEN), 1.0)
    hy_f_b1 = normal((DEPTH, HY_HIDDEN), 0.5)
    hy_f_w2 = dense((DEPTH, HY_HIDDEN, HY_HIDDEN), HY_HIDDEN)
    hy_f_b2 = normal((DEPTH, HY_HIDDEN), 0.1)
    hy_f_w3 = dense((DEPTH, HY_HIDDEN, HY_ORDER * 2 * HY_CH), HY_HIDDEN, 0.1)
    hy_f_b3 = normal((DEPTH, HY_ORDER * 2 * HY_CH), 0.01)
    hy_skip = normal((DEPTH, HY_ORDER, HY_CH), 0.5)
    dn_conv_w = dense((DEPTH, DN_CONV, 3 * GROUP_W), DN_CONV)
    dn_a_log = jnp.log(jax.random.uniform(next(keys), (DEPTH, 2, DN_HEADS), F32, 1.0, 16.0))
    dt = jax.random.uniform(next(keys), (DEPTH, 2, DN_HEADS), F32, 0.001, 0.1)
    dn_dt_bias = dt + jnp.log(-jnp.expm1(-dt))
    dn_norm_g = 1.0 + normal((DEPTH, DN_HEAD_DIM), 0.1)
    router_w = dense((DEPTH, D, N_EXPERTS), D)
    router_b = normal((DEPTH, N_EXPERTS), 0.01)
    moe_w_gu = dense((DEPTH, N_EXPERTS, D, 2 * D_FF), D)
    moe_b_gu = normal((DEPTH, N_EXPERTS, 2 * D_FF), 0.02)
    moe_w_down = dense((DEPTH, N_EXPERTS, D_FF, D), D_FF)
    moe_b_down = normal((DEPTH, N_EXPERTS, D), 0.02)
    return {'x': x, 'c': c, 'ctx': ctx, 'c_ctx': c_ctx, 'w_ada': w_ada, 'b_ada': b_ada, 'norm_g': norm_g,
            'w_in': w_in, 'w_out': w_out, 'na_rpb': na_rpb, 'ml_gate_b': ml_gate_b, 'ml_norm_g': ml_norm_g,
            'hy_conv_w': hy_conv_w, 'hy_conv_b': hy_conv_b, 'hy_f_w1': hy_f_w1, 'hy_f_b1': hy_f_b1,
            'hy_f_w2': hy_f_w2, 'hy_f_b2': hy_f_b2, 'hy_f_w3': hy_f_w3, 'hy_f_b3': hy_f_b3, 'hy_skip': hy_skip,
            'dn_conv_w': dn_conv_w, 'dn_a_log': dn_a_log, 'dn_dt_bias': dn_dt_bias, 'dn_norm_g': dn_norm_g,
            'router_w': router_w, 'router_b': router_b, 'moe_w_gu': moe_w_gu, 'moe_b_gu': moe_b_gu,
            'moe_w_down': moe_w_down, 'moe_b_down': moe_b_down}


def reference(x, c, ctx, c_ctx, w_ada, b_ada, norm_g, w_in, w_out, na_rpb, ml_gate_b, ml_norm_g, hy_conv_w,
              hy_conv_b, hy_f_w1, hy_f_b1, hy_f_w2, hy_f_b2, hy_f_w3, hy_f_b3, hy_skip, dn_conv_w, dn_a_log,
              dn_dt_bias, dn_norm_g, router_w, router_b, moe_w_gu, moe_b_gu, moe_w_down, moe_b_down):
    B, L, D = x.shape
    Lc = ctx.shape[1]
    xc = ctx
    for l in range(DEPTH):
        need_ctx = l < DEPTH - 1
        mod = (jax.nn.silu(c) @ w_ada[l] + b_ada[l])[:, None, :]
        mod_c = jax.nn.silu(c_ctx) @ w_ada[l] + b_ada[l]
        sh1, sc1, g1, sh2, sc2, g2 = jnp.split(mod, 6, axis=-1)
        csh1, csc1, cg1, csh2, csc2, cg2 = jnp.split(mod_c, 6, axis=-1)
        h = _rms_norm(x, norm_g[l, 0]) * (1.0 + sc1) + sh1
        hc = _rms_norm(xc, norm_g[l, 0]) * (1.0 + csc1) + csh1
        y, yc = _mixer_block(h, hc, w_in[l], w_out[l], na_rpb[l], ml_gate_b[l], ml_norm_g[l], hy_conv_w[l],
                             hy_conv_b[l], hy_f_w1[l], hy_f_b1[l], hy_f_w2[l], hy_f_b2[l], hy_f_w3[l], hy_f_b3[l],
                             hy_skip[l], dn_conv_w[l], dn_a_log[l], dn_dt_bias[l], dn_norm_g[l], need_ctx)
        x = x + g1 * _rms_norm(y, norm_g[l, 1])
        tokens = (_rms_norm(x, norm_g[l, 2]) * (1.0 + sc2) + sh2).reshape(B * L, D)
        if need_ctx:
            xc = xc + cg1 * _rms_norm(yc, norm_g[l, 1])
            hc2 = _rms_norm(xc, norm_g[l, 2]) * (1.0 + csc2) + csh2
            tokens = jnp.concatenate([tokens, hc2.reshape(B * Lc, D)], axis=0)
        f = _moe(tokens, router_w[l], router_b[l], moe_w_gu[l], moe_b_gu[l], moe_w_down[l], moe_b_down[l])
        x = x + g2 * _rms_norm(f[:B * L].reshape(B, L, D), norm_g[l, 3])
        if need_ctx:
            xc = xc + cg2 * _rms_norm(f[B * L:].reshape(B, Lc, D), norm_g[l, 3])
    return x
```

```python
import functools
import math

import numpy as np
import jax
import jax.numpy as jnp
from jax import lax
from jax.experimental import pallas as pl
from jax.experimental.pallas import tpu as pltpu

F32 = jnp.float32
BF16 = jnp.bfloat16
I32 = jnp.int32

D_MODEL = 2048
DEPTH = 2
GRID_W = 64
N_MIXERS = 4
GROUP_W = D_MODEL // N_MIXERS
NA_HEADS = 8
NA_HEAD_DIM = GROUP_W // NA_HEADS
NA_WIN_H = 8
NA_WIN_W = 16
ML_HEADS = 4
ML_HEAD_DIM = GROUP_W // ML_HEADS
HY_CH = GROUP_W
HY_BANDS = 8
HY_HIDDEN = 64
HY_SIN_FREQ = 1.0
HY_FAST_DECAY = 0.3
HY_SLOW_DECAY = 1.5
HY_DECAY_TARGET = 1e-2
DN_HEADS = 4
DN_HEAD_DIM = GROUP_W // DN_HEADS
CHUNK = 64
N_EXPERTS = 32
TOP_K = 4
D_FF = D_MODEL
SWIGLU_LIMIT = 7.0
SWIGLU_ALPHA = 1.702
MOE_BLOCK = 256
ROPE_BASE = 10000.0
EPS = 1e-6

LANE = 128
HALF = D_MODEL // 2
NEG = -1e30
HI = lax.Precision.HIGHEST

NA_Q, NA_K, NA_V = 0, 4, 8
ML_Q, ML_K, ML_V, ML_O = 12, 16, 20, 24
HY_U = 28
DN_Q = 40
DN_G = 52
SMALL = 56
P_COLS = 60 * LANE


def _params(sem, vmem_mb):
    return pltpu.CompilerParams(dimension_semantics=sem, vmem_limit_bytes=vmem_mb << 20)


def _mean_sq(x):
    return jnp.mean(x * x, axis=-1, keepdims=True)


def _sigmoid(x):
    return 1.0 / (1.0 + jnp.exp(-x))


def _adaln_body(c_ref, w_ref, b_ref, o_ref):
    c = c_ref[...]
    s = (c * _sigmoid(c)).astype(BF16)
    o_ref[0] = jnp.dot(s, w_ref[0].astype(BF16), preferred_element_type=F32) + b_ref[0]


def _adaln(cc, w_ada, b_ada):
    D = cc.shape[1]
    tn = 1024
    return pl.pallas_call(
        _adaln_body,
        grid=(DEPTH, 6 * D // tn),
        in_specs=[pl.BlockSpec((16, D), lambda l, j: (0, 0)),
                  pl.BlockSpec((1, D, tn), lambda l, j: (l, 0, j)),
                  pl.BlockSpec((1, 1, tn), lambda l, j: (l, 0, j))],
        out_specs=pl.BlockSpec((1, 16, tn), lambda l, j: (l, 0, j)),
        out_shape=jax.ShapeDtypeStruct((DEPTH, 16, 6 * D), F32),
        compiler_params=_params(("arbitrary", "arbitrary"), 40),
        name="adaln",
    )(cc, w_ada, b_ada.reshape(DEPTH, 1, 6 * D))


def _inproj_body(x_ref, g_ref, sc_ref, sh_ref, w_ref, o_ref, h_scr):
    @pl.when(pl.program_id(1) == 0)
    def _():
        x = x_ref[...]
        y = x * lax.rsqrt(_mean_sq(x) + EPS) * g_ref[...]
        h_scr[...] = (y * (1.0 + sc_ref[0, 0]) + sh_ref[0, 0]).astype(BF16)

    o_ref[...] = jnp.dot(h_scr[...], w_ref[...], preferred_element_type=F32)


def _inproj(x2, seq_len, mod, g, w_p):
    T, D = x2.shape
    tm = min(1024, seq_len)
    tn = 512
    per_seq = seq_len // tm
    bmap = (lambda i: i // per_seq) if mod.shape[0] > 1 else (lambda i: 0)
    return pl.pallas_call(
        _inproj_body,
        grid=(T // tm, P_COLS // tn),
        in_specs=[pl.BlockSpec((tm, D), lambda i, j: (i, 0)),
                  pl.BlockSpec((1, D), lambda i, j: (0, 0)),
                  pl.BlockSpec((1, 1, 1, D), lambda i, j: (bmap(i), 1, 0, 0)),
                  pl.BlockSpec((1, 1, 1, D), lambda i, j: (bmap(i), 0, 0, 0)),
                  pl.BlockSpec((D, tn), lambda i, j: (0, j))],
        out_specs=pl.BlockSpec((tm, tn), lambda i, j: (i, j)),
        out_shape=jax.ShapeDtypeStruct((T, P_COLS), F32),
        scratch_shapes=[pltpu.VMEM((tm, D), BF16)],
        compiler_params=_params(("arbitrary", "arbitrary"), 48),
        name="inproj",
    )(x2, g, mod, mod, w_p)


def _pack_halves(t):
    hi = pltpu.bitcast(t[:, :HALF].astype(BF16).astype(F32), I32)
    lo = pltpu.bitcast(t[:, HALF:].astype(BF16).astype(F32), I32)
    return hi | lax.shift_right_logical(lo, 16)


def _unpack_halves(u):
    hi = pltpu.bitcast(u & jnp.int32(-65536), F32)
    lo = pltpu.bitcast(lax.shift_left(u, 16), F32)
    return hi, lo


def _outproj_body(ya_ref, yb_ref, yh_ref, yd_ref, w_ref, x_ref, g1n_ref, gate_ref, g2n_ref, sc_ref, sh_ref,
                  rw_ref, rb_ref, xo_ref, tok_ref, lg_ref):
    G = GROUP_W
    y = jnp.dot(ya_ref[...].astype(BF16), w_ref[0:G, :], preferred_element_type=F32)
    y += jnp.dot(yb_ref[...].astype(BF16), w_ref[G:2 * G, :], preferred_element_type=F32)
    y += jnp.dot(yh_ref[...].astype(BF16), w_ref[2 * G:3 * G, :], preferred_element_type=F32)
    y += jnp.dot(yd_ref[...].astype(BF16), w_ref[3 * G:4 * G, :], preferred_element_type=F32)
    yn = y * lax.rsqrt(_mean_sq(y) + EPS) * g1n_ref[...]
    xn = x_ref[...] + gate_ref[0, 0] * yn
    xo_ref[...] = xn
    t = xn * lax.rsqrt(_mean_sq(xn) + EPS) * g2n_ref[...]
    t = t * (1.0 + sc_ref[0, 0]) + sh_ref[0, 0]
    tok_ref[...] = _pack_halves(t)
    lg_ref[...] = jnp.dot(t, rw_ref[...], preferred_element_type=F32, precision=HI) + rb_ref[...]


def _outproj(ys, w_out_bf, x2, seq_len, mod, g1n, g2n, router_w, router_b):
    T, D = x2.shape
    tm = 256
    per_seq = seq_len // tm
    bmap = (lambda i: i // per_seq) if mod.shape[0] > 1 else (lambda i: 0)
    row = lambda i: (i, 0)
    fix = lambda i: (0, 0)
    modspec = lambda k: pl.BlockSpec((1, 1, 1, D), lambda i: (bmap(i), k, 0, 0))
    return pl.pallas_call(
        _outproj_body,
        grid=(T // tm,),
        in_specs=[pl.BlockSpec((tm, GROUP_W), row)] * 4 + [
            pl.BlockSpec((D, D), fix), pl.BlockSpec((tm, D), row), pl.BlockSpec((1, D), fix), modspec(2),
            pl.BlockSpec((1, D), fix), modspec(4), modspec(3),
            pl.BlockSpec((D, N_EXPERTS), fix), pl.BlockSpec((1, N_EXPERTS), fix)],
        out_specs=[pl.BlockSpec((tm, D), row), pl.BlockSpec((tm, HALF), row), pl.BlockSpec((tm, N_EXPERTS), row)],
        out_shape=[jax.ShapeDtypeStruct((T, D), F32), jax.ShapeDtypeStruct((T, HALF), I32),
                   jax.ShapeDtypeStruct((T, N_EXPERTS), F32)],
        compiler_params=_params(("arbitrary",), 48),
        name="outproj",
    )(*ys, w_out_bf, x2, g1n, mod, g2n, mod, mod, router_w, router_b)


def _lane_select(cols, width, shape):
    lane = lax.broadcasted_iota(I32, shape, 1)
    out = jnp.broadcast_to(cols[width - 1], shape)
    for k in range(width - 2, -1, -1):
        out = jnp.where(lane == k, cols[k], out)
    return out


def _router_body(l_ref, idx_ref, rank_ref, gate_ref, cnt_ref, carry):
    @pl.when(pl.program_id(0) == 0)
    def _():
        carry[...] = jnp.zeros_like(carry)

    cur = l_ref[...]
    tm = cur.shape[0]
    lane = lax.broadcasted_iota(I32, cur.shape, 1)
    vals, ids, hots = [], [], []
    for _ in range(TOP_K):
        m = jnp.max(cur, axis=-1, keepdims=True)
        sel = jnp.min(jnp.where(cur == m, lane, N_EXPERTS), axis=-1, keepdims=True)
        hot = lane == sel
        vals.append(m)
        ids.append(sel)
        hots.append(hot)
        cur = jnp.where(hot, -jnp.inf, cur)
    es = [jnp.exp(v - vals[0]) for v in vals]
    tot = es[0] + es[1] + es[2] + es[3]
    cnt = sum(h.astype(F32) for h in hots)
    r = lax.broadcasted_iota(I32, (tm, tm), 0)
    c = lax.broadcasted_iota(I32, (tm, tm), 1)
    before = (c < r).astype(BF16)
    pre = jnp.dot(before, cnt.astype(BF16), preferred_element_type=F32) + carry[...]
    ranks = [jnp.sum(jnp.where(h, pre, 0.0), axis=-1, keepdims=True).astype(I32) for h in hots]
    carry[...] += jnp.sum(cnt, axis=0, keepdims=True)
    shape = (tm, TOP_K)
    idx_ref[...] = _lane_select(ids, TOP_K, shape)
    rank_ref[...] = _lane_select(ranks, TOP_K, shape)
    gate_ref[...] = _lane_select([e / tot for e in es], TOP_K, shape)
    cnt_ref[...] = carry[...].astype(I32)


def _router(logits):
    T = logits.shape[0]
    tm = 256
    row = lambda i: (i, 0)
    return pl.pallas_call(
        _router_body,
        grid=(T // tm,),
        in_specs=[pl.BlockSpec((tm, N_EXPERTS), row)],
        out_specs=[pl.BlockSpec((tm, TOP_K), row), pl.BlockSpec((tm, TOP_K), row), pl.BlockSpec((tm, TOP_K), row),
                   pl.BlockSpec((1, N_EXPERTS), lambda i: (0, 0))],
        out_shape=[jax.ShapeDtypeStruct((T, TOP_K), I32), jax.ShapeDtypeStruct((T, TOP_K), I32),
                   jax.ShapeDtypeStruct((T, TOP_K), F32), jax.ShapeDtypeStruct((1, N_EXPERTS), I32)],
        scratch_shapes=[pltpu.VMEM((1, N_EXPERTS), F32)],
        compiler_params=_params(("arbitrary",), 32),
        name="router",
    )(logits)


MOE_TM = 256


def _slot_copy(src_ref, dst_ref, src_row, dst_row, sem):
    return pltpu.make_async_copy(src_ref.at[pl.ds(src_row, 1)], dst_ref.at[pl.ds(dst_row, 1)], sem)


def _dispatch_body(ps_ref, idx_ref, rank_ref, tok_ref, xs_in_ref, xs_ref, sem):
    del xs_in_ref

    def start(r, carry):
        for k in range(TOP_K):
            d = ps_ref[idx_ref[0, 0, r * TOP_K + k]] + rank_ref[0, 0, r * TOP_K + k]
            _slot_copy(tok_ref, xs_ref, r, d, sem).start()
        return carry

    lax.fori_loop(0, MOE_TM, start, 0)

    def wait(r, carry):
        for k in range(TOP_K):
            _slot_copy(tok_ref, xs_ref, 0, 0, sem).wait()
        return carry

    lax.fori_loop(0, MOE_TM, wait, 0)


def _dispatch(pad_start, idx3, rank3, tok, n_slots):
    T = tok.shape[0]
    tm = MOE_TM
    smem = lambda: pl.BlockSpec((1, 1, tm * TOP_K), lambda i, ps: (i, 0, 0), memory_space=pltpu.SMEM)
    return pl.pallas_call(
        _dispatch_body,
        grid_spec=pltpu.PrefetchScalarGridSpec(
            num_scalar_prefetch=1,
            grid=(T // tm,),
            in_specs=[smem(), smem(), pl.BlockSpec((tm, HALF), lambda i, ps: (i, 0)),
                      pl.BlockSpec(memory_space=pl.ANY)],
            out_specs=pl.BlockSpec(memory_space=pl.ANY),
            scratch_shapes=[pltpu.SemaphoreType.DMA(())]),
        out_shape=jax.ShapeDtypeStruct((n_slots, HALF), I32),
        input_output_aliases={4: 0},
        compiler_params=_params(("arbitrary",), 32),
        name="moe_dispatch",
    )(pad_start, idx3, rank3, tok, jnp.zeros((n_slots, HALF), I32))


def _g1_body(be_ref, nu_ref, x_ref, wg_ref, wu_ref, bg_ref, bu_ref, h_ref):
    used = pl.program_id(1) < nu_ref[0]

    @pl.when(used)
    def _():
        hi, lo = _unpack_halves(x_ref[...])
        hi = hi.astype(BF16)
        lo = lo.astype(BF16)

        def proj(w_ref, b_ref):
            return (jnp.dot(hi, w_ref[0, :HALF, :], preferred_element_type=F32)
                    + jnp.dot(lo, w_ref[0, HALF:, :], preferred_element_type=F32) + b_ref[0])

        g = jnp.minimum(proj(wg_ref, bg_ref), SWIGLU_LIMIT)
        u = jnp.clip(proj(wu_ref, bu_ref), -SWIGLU_LIMIT, SWIGLU_LIMIT)
        h_ref[...] = (g * _sigmoid(SWIGLU_ALPHA * g) * (u + 1.0)).astype(BF16)

    @pl.when(jnp.logical_not(used))
    def _():
        h_ref[...] = jnp.zeros_like(h_ref)


def _expert_up(block_expert, n_used, xs, w_gu_bf, b_gu3):
    n_slots = xs.shape[0]
    nb = n_slots // MOE_BLOCK
    tn = 1024
    nj = D_FF // tn
    last = lambda i, nu: jnp.minimum(i, nu[0] - 1)
    return pl.pallas_call(
        _g1_body,
        grid_spec=pltpu.PrefetchScalarGridSpec(
            num_scalar_prefetch=2,
            grid=(nj, nb),
            in_specs=[pl.BlockSpec((MOE_BLOCK, HALF), lambda j, i, be, nu: (last(i, nu), 0)),
                      pl.BlockSpec((1, D_MODEL, tn), lambda j, i, be, nu: (be[last(i, nu)], 0, j)),
                      pl.BlockSpec((1, D_MODEL, tn), lambda j, i, be, nu: (be[last(i, nu)], 0, nj + j)),
                      pl.BlockSpec((1, 1, tn), lambda j, i, be, nu: (be[last(i, nu)], 0, j)),
                      pl.BlockSpec((1, 1, tn), lambda j, i, be, nu: (be[last(i, nu)], 0, nj + j))],
            out_specs=pl.BlockSpec((MOE_BLOCK, tn), lambda j, i, be, nu: (i, j))),
        out_shape=jax.ShapeDtypeStruct((n_slots, D_FF), BF16),
        compiler_params=_params(("arbitrary", "arbitrary"), 48),
        name="moe_up",
    )(block_expert, n_used, xs, w_gu_bf, w_gu_bf, b_gu3, b_gu3)


def _g2_body(be_ref, nu_ref, h_ref, w_ref, b_ref, y_ref):
    used = pl.program_id(0) < nu_ref[0]

    @pl.when(used)
    def _():
        y_ref[...] = _pack_halves(jnp.dot(h_ref[...], w_ref[0], preferred_element_type=F32) + b_ref[0])

    @pl.when(jnp.logical_not(used))
    def _():
        y_ref[...] = jnp.zeros_like(y_ref)


def _expert_down(block_expert, n_used, h, w_down_bf, b_down3):
    n_slots = h.shape[0]
    nb = n_slots // MOE_BLOCK
    last = lambda i, nu: jnp.minimum(i, nu[0] - 1)
    return pl.pallas_call(
        _g2_body,
        grid_spec=pltpu.PrefetchScalarGridSpec(
            num_scalar_prefetch=2,
            grid=(nb,),
            in_specs=[pl.BlockSpec((MOE_BLOCK, D_FF), lambda i, be, nu: (last(i, nu), 0)),
                      pl.BlockSpec((1, D_FF, D_MODEL), lambda i, be, nu: (be[last(i, nu)], 0, 0)),
                      pl.BlockSpec((1, 1, D_MODEL), lambda i, be, nu: (be[last(i, nu)], 0, 0))],
            out_specs=pl.BlockSpec((MOE_BLOCK, HALF), lambda i, be, nu: (i, 0))),
        out_shape=jax.ShapeDtypeStruct((n_slots, HALF), I32),
        compiler_params=_params(("arbitrary",), 48),
        name="moe_down",
    )(block_expert, n_used, h, w_down_bf, b_down3)


def _combine_body(ps_ref, idx_ref, rank_ref, gate_ref, ys_ref, x_ref, gn_ref, g2_ref, o_ref, buf, sem):
    def start(r, carry):
        for k in range(TOP_K):
            d = ps_ref[idx_ref[0, 0, r * TOP_K + k]] + rank_ref[0, 0, r * TOP_K + k]
            _slot_copy(ys_ref, buf.at[k], d, r, sem).start()
        return carry

    lax.fori_loop(0, MOE_TM, start, 0)

    def wait(r, carry):
        for k in range(TOP_K):
            _slot_copy(ys_ref, buf.at[k], 0, 0, sem).wait()
        return carry

    lax.fori_loop(0, MOE_TM, wait, 0)

    gate = gate_ref[...]
    f_hi = f_lo = None
    for k in range(TOP_K):
        hi, lo = _unpack_halves(buf[k])
        gk = gate[:, k:k + 1]
        f_hi = gk * hi if k == 0 else f_hi + gk * hi
        f_lo = gk * lo if k == 0 else f_lo + gk * lo
    ms = (jnp.sum(f_hi * f_hi, axis=-1, keepdims=True) + jnp.sum(f_lo * f_lo, axis=-1, keepdims=True)) / D_MODEL
    scale = lax.rsqrt(ms + EPS)
    gn = gn_ref[...]
    g2 = g2_ref[0, 0]
    o_ref[:, :HALF] = x_ref[:, :HALF] + g2[:, :HALF] * (f_hi * scale * gn[:, :HALF])
    o_ref[:, HALF:] = x_ref[:, HALF:] + g2[:, HALF:] * (f_lo * scale * gn[:, HALF:])


def _combine(pad_start, idx3, rank3, gates, ys, x2, seq_len, mod, gn):
    T, D = x2.shape
    tm = MOE_TM
    per_seq = seq_len // tm
    bmap = (lambda i: i // per_seq) if mod.shape[0] > 1 else (lambda i: 0)
    smem = lambda: pl.BlockSpec((1, 1, tm * TOP_K), lambda i, ps: (i, 0, 0), memory_space=pltpu.SMEM)
    return pl.pallas_call(
        _combine_body,
        grid_spec=pltpu.PrefetchScalarGridSpec(
            num_scalar_prefetch=1,
            grid=(T // tm,),
            in_specs=[smem(), smem(), pl.BlockSpec((tm, TOP_K), lambda i, ps: (i, 0)),
                      pl.BlockSpec(memory_space=pl.ANY),
                      pl.BlockSpec((tm, D), lambda i, ps: (i, 0)),
                      pl.BlockSpec((1, D), lambda i, ps: (0, 0)),
                      pl.BlockSpec((1, 1, 1, D), lambda i, ps: (bmap(i), 5, 0, 0))],
            out_specs=pl.BlockSpec((tm, D), lambda i, ps: (i, 0)),
            scratch_shapes=[pltpu.VMEM((TOP_K, tm, HALF), I32), pltpu.SemaphoreType.DMA(())]),
        out_shape=jax.ShapeDtypeStruct((T, D), F32),
        compiler_params=_params(("arbitrary",), 40),
        name="moe_combine",
    )(pad_start, idx3, rank3, gates, ys, x2, gn, mod)


def _moe(tok, logits, w_gu_bf, b_gu, w_down_bf, b_down):
    T = tok.shape[0]
    idx, rank, gates, counts = _router(logits)
    counts = counts[0]
    padded = (counts + MOE_BLOCK - 1) // MOE_BLOCK * MOE_BLOCK
    pad_end = jnp.cumsum(padded)
    pad_start = (pad_end - padded).astype(I32)
    n_slots = (T * TOP_K // MOE_BLOCK + N_EXPERTS) * MOE_BLOCK
    nb = n_slots // MOE_BLOCK
    blk = jnp.arange(nb, dtype=I32) * MOE_BLOCK
    block_expert = jnp.minimum(jnp.sum((blk[:, None] >= pad_end[None, :]).astype(I32), axis=1), N_EXPERTS - 1)
    n_used = (pad_end[-1:] // MOE_BLOCK).astype(I32)
    idx3 = idx.reshape(T // MOE_TM, 1, MOE_TM * TOP_K)
    rank3 = rank.reshape(T // MOE_TM, 1, MOE_TM * TOP_K)
    xs = _dispatch(pad_start, idx3, rank3, tok, n_slots)
    h = _expert_up(block_expert, n_used, xs, w_gu_bf, b_gu.reshape(N_EXPERTS, 1, 2 * D_FF))
    ys = _expert_down(block_expert, n_used, h, w_down_bf, b_down.reshape(N_EXPERTS, 1, D_MODEL))
    return pad_start, idx3, rank3, gates, ys


NA_QT = 4
NA_BAND = 12
NA_ROWS = 32


def _na_bias_table(rpb):
    W = GRID_W
    cols = np.arange(W)
    cs = np.clip(cols - NA_WIN_W // 2, 0, W - NA_WIN_W)
    valid_c = (cols[None, :] >= cs[:, None]) & (cols[None, :] < cs[:, None] + NA_WIN_W)
    dc = np.clip(cols[None, :] - cols[:, None] + NA_WIN_W - 1, 0, 2 * NA_WIN_W - 2)
    tm = jnp.where(valid_c, rpb[:, :, dc], NEG)
    nt = NA_ROWS // NA_QT
    qt = np.arange(nt)
    r = NA_QT * qt[:, None] + np.arange(NA_QT)[None, :]
    kb = np.clip(NA_QT * qt - NA_WIN_H // 2, 0, NA_ROWS - NA_BAND)
    kr = kb[:, None] + np.arange(NA_BAND)[None, :]
    rs = np.clip(r - NA_WIN_H // 2, 0, NA_ROWS - NA_WIN_H)
    valid_r = (kr[:, None, :] >= rs[:, :, None]) & (kr[:, None, :] < rs[:, :, None] + NA_WIN_H)
    dr = np.clip(kr[:, None, :] - r[:, :, None] + NA_WIN_H - 1, 0, 2 * NA_WIN_H - 2)
    blocks = jnp.where(valid_r[None, :, :, :, None, None], tm[:, dr], NEG)
    return blocks.transpose(1, 0, 2, 4, 3, 5).reshape(nt, NA_HEADS, NA_QT * W, NA_BAND * W)


def _masked_heads_attention(q, keys, values, biases):
    lane = lax.broadcasted_iota(I32, q.shape, 1)
    nt = (((1,), (1,)), ((), ()))
    out = None
    for hh in range(2):
        hm = (lane >= NA_HEAD_DIM) == bool(hh)
        qh = jnp.where(hm, q, 0.0).astype(BF16)
        ss = []
        for kk, bb in zip(keys, biases):
            s = lax.dot_general(qh, kk, nt, preferred_element_type=F32)
            ss.append(s if bb is None else s + bb[hh])
        m = ss[0].max(axis=1, keepdims=True)
        for s in ss[1:]:
            m = jnp.maximum(m, s.max(axis=1, keepdims=True))
        den = None
        o = None
        for s, vv in zip(ss, values):
            p = jnp.exp(s - m)
            d = jnp.sum(p, axis=1, keepdims=True)
            den = d if den is None else den + d
            pv = jnp.dot(p.astype(BF16), vv, preferred_element_type=F32)
            o = pv if o is None else o + pv
        o = o / den
        out = o if hh == 0 else jnp.where(hm, o, out)
    return out


def _na_body(q_ref, k_ref, v_ref, kc_ref, vc_ref, b_ref, o_ref):
    qt = pl.program_id(2)
    kb = pl.multiple_of(jnp.clip(qt * NA_QT - NA_WIN_H // 2, 0, NA_ROWS - NA_BAND) * GRID_W, GRID_W)
    band = NA_BAND * GRID_W
    q = q_ref[0] * (NA_HEAD_DIM ** -0.5)
    kl = k_ref[0, pl.ds(kb, band), :].astype(BF16)
    vl = v_ref[0, pl.ds(kb, band), :].astype(BF16)
    o_ref[0] = _masked_heads_attention(q, [kl, kc_ref[0].astype(BF16)], [vl, vc_ref[0].astype(BF16)],
                                       [(b_ref[0, 0], b_ref[0, 1]), None])


def _na_attention(p, pc, bias):
    B, L, _ = p.shape
    Lc = pc.shape[1]
    assert L == NA_ROWS * GRID_W
    tq = NA_QT * GRID_W
    return pl.pallas_call(
        _na_body,
        grid=(B, NA_HEADS // 2, L // tq),
        in_specs=[pl.BlockSpec((1, tq, LANE), lambda b, h, t: (b, t, NA_Q + h)),
                  pl.BlockSpec((1, L, LANE), lambda b, h, t: (b, 0, NA_K + h)),
                  pl.BlockSpec((1, L, LANE), lambda b, h, t: (b, 0, NA_V + h)),
                  pl.BlockSpec((1, Lc, LANE), lambda b, h, t: (b, 0, NA_K + h)),
                  pl.BlockSpec((1, Lc, LANE), lambda b, h, t: (b, 0, NA_V + h)),
                  pl.BlockSpec((1, 2, tq, NA_BAND * GRID_W), lambda b, h, t: (t, h, 0, 0))],
        out_specs=pl.BlockSpec((1, tq, LANE), lambda b, h, t: (b, t, h)),
        out_shape=jax.ShapeDtypeStruct((B, L, GROUP_W), F32),
        compiler_params=_params(("arbitrary", "arbitrary", "arbitrary"), 40),
        name="na_attention",
    )(p, p, p, pc, pc, bias)


def _ctx_attn_body(q_ref, k_ref, v_ref, o_ref):
    q = q_ref[0] * (NA_HEAD_DIM ** -0.5)
    o_ref[0] = _masked_heads_attention(q, [k_ref[0].astype(BF16)], [v_ref[0].astype(BF16)], [None])


def _ctx_attention(pc):
    B, Lc, _ = pc.shape
    return pl.pallas_call(
        _ctx_attn_body,
        grid=(B, NA_HEADS // 2),
        in_specs=[pl.BlockSpec((1, Lc, LANE), lambda b, h: (b, 0, NA_Q + h)),
                  pl.BlockSpec((1, Lc, LANE), lambda b, h: (b, 0, NA_K + h)),
                  pl.BlockSpec((1, Lc, LANE), lambda b, h: (b, 0, NA_V + h))],
        out_specs=pl.BlockSpec((1, Lc, LANE), lambda b, h: (b, 0, h)),
        out_shape=jax.ShapeDtypeStruct((B, Lc, GROUP_W), F32),
        compiler_params=_params(("arbitrary", "arbitrary"), 32),
        name="ctx_attention",
    )(pc, pc, pc)


CONV_TT = 256
HALO = 8


def _conv3_body(xp_ref, x_ref, xn_ref, w_ref, b_ref, o_ref, *, silu, nt):
    i = pl.program_id(1)
    x = x_ref[0]
    tt = x.shape[0]
    prev_row = jnp.where(i == 0, 0.0, xp_ref[0, HALO - 1:HALO, :])
    next_row = jnp.where(i == nt - 1, 0.0, xn_ref[0, 0:1, :])
    row = lax.broadcasted_iota(I32, x.shape, 0)
    xm = jnp.where(row == 0, prev_row, pltpu.roll(x, 1, 0))
    xp = jnp.where(row == tt - 1, next_row, pltpu.roll(x, tt - 1, 0))
    y = xm * w_ref[0:1, :] + x * w_ref[1:2, :] + xp * w_ref[2:3, :] + b_ref[...]
    o_ref[0] = y * _sigmoid(y) if silu else y


def _conv3(p, col0, w, b, silu):
    B, L, _ = p.shape
    C = w.shape[1]
    cw = 512
    tt = min(CONV_TT, L)
    nt = L // tt
    hb = tt // HALO
    cb = col0 * LANE // cw
    assert col0 * LANE % cw == 0
    return pl.pallas_call(
        functools.partial(_conv3_body, silu=silu, nt=nt),
        grid=(B, nt, C // cw),
        in_specs=[pl.BlockSpec((1, HALO, cw), lambda bb, i, j: (bb, jnp.maximum(i * hb - 1, 0), cb + j)),
                  pl.BlockSpec((1, tt, cw), lambda bb, i, j: (bb, i, cb + j)),
                  pl.BlockSpec((1, HALO, cw), lambda bb, i, j: (bb, jnp.minimum((i + 1) * hb, L // HALO - 1), cb + j)),
                  pl.BlockSpec((3, cw), lambda bb, i, j: (0, j)),
                  pl.BlockSpec((1, cw), lambda bb, i, j: (0, j))],
        out_specs=pl.BlockSpec((1, tt, cw), lambda bb, i, j: (bb, i, j)),
        out_shape=jax.ShapeDtypeStruct((B, L, C), F32),
        compiler_params=_params(("arbitrary", "arbitrary", "arbitrary"), 32),
        name="conv3",
    )(p, p, p, w, b)


def _dft_tables(L):
    N = 2 * L
    f = jnp.arange(L, dtype=I32)[:, None]
    t = jnp.arange(L, dtype=I32)[None, :]
    ang = ((f * t) % N).astype(F32) * (2.0 * math.pi / N)
    c, s = jnp.cos(ang), jnp.sin(ang)
    nyq = jnp.where(t % 2 == 0, 1.0, -1.0).astype(F32)
    dc = f == 0
    fc = c
    fs = jnp.where(dc, nyq, -s)
    ci = (jnp.where(dc, 1.0, 2.0) * c / N).T
    si = (jnp.where(dc, nyq, -2.0 * s) / N).T
    return fc.astype(BF16), fs.astype(BF16), ci.astype(BF16), si.astype(BF16)


def _hy_filter_body(w1_ref, b1_ref, w2_ref, b2_ref, w3_ref, b3_ref, win_ref, o_ref, *, L, tt):
    i = pl.program_id(0)
    pos = (lax.broadcasted_iota(I32, (tt, LANE), 0) + i * tt).astype(F32) / L
    lane = lax.broadcasted_iota(I32, (tt, LANE), 1)
    band = jnp.where(lane <= HY_BANDS, lane, lane - HY_BANDS).astype(F32)
    ang = (2.0 * math.pi) * pos * band
    feats = jnp.where(lane == 0, pos, jnp.where(lane <= HY_BANDS, jnp.sin(ang),
                                                jnp.where(lane <= 2 * HY_BANDS, jnp.cos(ang), 0.0)))
    h = jnp.sin(HY_SIN_FREQ * (jnp.dot(feats, w1_ref[...], preferred_element_type=F32, precision=HI) + b1_ref[...]))
    h = jnp.sin(HY_SIN_FREQ * (jnp.dot(h, w2_ref[...], preferred_element_type=F32, precision=HI) + b2_ref[...]))
    h = jnp.dot(h, w3_ref[...], preferred_element_type=F32, precision=HI) + b3_ref[...]
    t1 = (lax.broadcasted_iota(I32, (tt, 1), 0) + i * tt).astype(F32) / L
    h = h * jnp.exp(-t1 * win_ref[...])
    row = lax.broadcasted_iota(I32, h.shape, 0) + i * tt
    col = lax.broadcasted_iota(I32, h.shape, 1)
    bwd = (col // HY_CH) % 2 == 1
    o_ref[...] = jnp.where(jnp.logical_and(row == 0, bwd), 0.0, h)


def _hy_filters(L, w1, b1, w2, b2, w3, b3):
    tt = min(256, L)
    nc = 4 * HY_CH
    w1p = jnp.zeros((LANE, HY_HIDDEN), F32).at[:w1.shape[0]].set(w1)
    max_decay = math.log(HY_DECAY_TARGET) / HY_FAST_DECAY
    min_decay = math.log(HY_DECAY_TARGET) / HY_SLOW_DECAY
    deltas = jnp.abs(jnp.linspace(min_decay, max_decay, HY_CH, dtype=F32))
    win = jnp.tile(deltas, 4)[None, :]
    fix = lambda i: (0, 0)
    return pl.pallas_call(
        functools.partial(_hy_filter_body, L=L, tt=tt),
        grid=(L // tt,),
        in_specs=[pl.BlockSpec((LANE, HY_HIDDEN), fix), pl.BlockSpec((1, HY_HIDDEN), fix),
                  pl.BlockSpec((HY_HIDDEN, HY_HIDDEN), fix), pl.BlockSpec((1, HY_HIDDEN), fix),
                  pl.BlockSpec((HY_HIDDEN, nc), fix), pl.BlockSpec((1, nc), fix), pl.BlockSpec((1, nc), fix)],
        out_specs=pl.BlockSpec((tt, nc), lambda i: (i, 0)),
        out_shape=jax.ShapeDtypeStruct((L, nc), F32),
        compiler_params=_params(("arbitrary",), 32),
        name="hyena_filters",
    )(w1p, b1[None, :], w2, b2[None, :], w3, b3[None, :], win)


def _hy_spectrum_body(fc_ref, fs_ref, h_ref, gc_ref, gs_ref):
    hb = h_ref[...].astype(BF16)
    gc_ref[...] = jnp.dot(fc_ref[...], hb, preferred_element_type=F32)
    gs_ref[...] = jnp.dot(fs_ref[...], hb, preferred_element_type=F32)


def _hy_spectrum(fc, fs, hmat):
    L, nc = hmat.shape
    tf = min(512, L)
    tn = 512
    return pl.pallas_call(
        _hy_spectrum_body,
        grid=(nc // tn, L // tf),
        in_specs=[pl.BlockSpec((tf, L), lambda j, i: (i, 0)), pl.BlockSpec((tf, L), lambda j, i: (i, 0)),
                  pl.BlockSpec((L, tn), lambda j, i: (0, j))],
        out_specs=[pl.BlockSpec((tf, tn), lambda j, i: (i, j)), pl.BlockSpec((tf, tn), lambda j, i: (i, j))],
        out_shape=[jax.ShapeDtypeStruct((L, nc), F32), jax.ShapeDtypeStruct((L, nc), F32)],
        compiler_params=_params(("arbitrary", "arbitrary"), 40),
        name="hyena_spectrum",
    )(fc, fs, hmat)


def _hy_fwd_body(fc_ref, fs_ref, z_ref, gc_ref, gs_ref, yre_ref, yim_ref, zb):
    i = pl.program_id(1)

    @pl.when(i == 0)
    def _():
        zb[...] = z_ref[0].astype(BF16)

    C = HY_CH
    zre = jnp.dot(fc_ref[...], zb[...], preferred_element_type=F32)
    zim = jnp.dot(fs_ref[...], zb[...], preferred_element_type=F32)
    gre = gc_ref[:, :C] + gc_ref[:, C:]
    a = gs_ref[:, :C]
    b = gs_ref[:, C:]
    packed = jnp.logical_and(lax.broadcasted_iota(I32, zre.shape, 0) == 0, i == 0)
    gim = jnp.where(packed, a + b, a - b)
    yre_ref[0] = jnp.where(packed, zre * gre, zre * gre - zim * gim).astype(BF16)
    yim_ref[0] = jnp.where(packed, zim * gim, zre * gim + zim * gre).astype(BF16)


def _hy_forward(fc, fs, z_src, z_col, gc, gs, order):
    B, L, _ = z_src.shape
    C = HY_CH
    tf = min(512, L)
    return pl.pallas_call(
        _hy_fwd_body,
        grid=(B, L // tf),
        in_specs=[pl.BlockSpec((tf, L), lambda b, i: (i, 0)), pl.BlockSpec((tf, L), lambda b, i: (i, 0)),
                  pl.BlockSpec((1, L, C), lambda b, i: (b, 0, z_col)),
                  pl.BlockSpec((tf, 2 * C), lambda b, i: (i, order)), pl.BlockSpec((tf, 2 * C), lambda b, i: (i, order))],
        out_specs=[pl.BlockSpec((1, tf, C), lambda b, i: (b, i, 0)), pl.BlockSpec((1, tf, C), lambda b, i: (b, i, 0))],
        out_shape=[jax.ShapeDtypeStruct((B, L, C), BF16), jax.ShapeDtypeStruct((B, L, C), BF16)],
        scratch_shapes=[pltpu.VMEM((L, C), BF16)],
        compiler_params=_params(("arbitrary", "arbitrary"), 40),
        name="hyena_forward",
    )(fc, fs, z_src, gc, gs)


def _hy_inv_body(ci_ref, si_ref, yre_ref, yim_ref, z_ref, x_ref, skip_ref, o_ref):
    y = jnp.dot(ci_ref[...], yre_ref[0], preferred_element_type=F32)
    y += jnp.dot(si_ref[...], yim_ref[0], preferred_element_type=F32)
    o_ref[0] = x_ref[0] * (y + z_ref[0] * skip_ref[0])


def _hy_inverse(ci, si, yre, yim, z_src, z_col, x_src, x_col, skip, order):
    B, L, C = yre.shape
    tt = min(512, L)
    return pl.pallas_call(
        _hy_inv_body,
        grid=(B, L // tt),
        in_specs=[pl.BlockSpec((tt, L), lambda b, i: (i, 0)), pl.BlockSpec((tt, L), lambda b, i: (i, 0)),
                  pl.BlockSpec((1, L, C), lambda b, i: (b, 0, 0)), pl.BlockSpec((1, L, C), lambda b, i: (b, 0, 0)),
                  pl.BlockSpec((1, tt, C), lambda b, i: (b, i, z_col)),
                  pl.BlockSpec((1, tt, C), lambda b, i: (b, i, x_col)),
                  pl.BlockSpec((1, 1, C), lambda b, i: (order, 0, 0))],
        out_specs=pl.BlockSpec((1, tt, C), lambda b, i: (b, i, 0)),
        out_shape=jax.ShapeDtypeStruct((B, L, C), F32),
        compiler_params=_params(("arbitrary", "arbitrary"), 40),
        name="hyena_inverse",
    )(ci, si, yre, yim, z_src, x_src, skip)


def _hyena(p, conv_w, conv_b, w1, b1, w2, b2, w3, b3, skip):
    L = p.shape[1]
    u = _conv3(p, HY_U, conv_w, conv_b[None, :], silu=False)
    fc, fs, ci, si = _dft_tables(L)
    gc, gs = _hy_spectrum(fc, fs, _hy_filters(L, w1, b1, w2, b2, w3, b3))
    skip3 = skip[:, None, :]
    yre, yim = _hy_forward(fc, fs, u, 2, gc, gs, 0)
    z1 = _hy_inverse(ci, si, yre, yim, u, 2, u, 0, skip3, 0)
    yre, yim = _hy_forward(fc, fs, z1, 0, gc, gs, 1)
    return _hy_inverse(ci, si, yre, yim, z1, 0, u, 1, skip3, 1)


NT_DIMS = (((1,), (1,)), ((), ()))
TN_DIMS = (((0,), (0,)), ((), ()))


def _chunk_masks(reverse):
    r = lax.broadcasted_iota(I32, (CHUNK, CHUNK), 0)
    c = lax.broadcasted_iota(I32, (CHUNK, CHUNK), 1)
    if reverse:
        return (r >= c).astype(F32), c >= r, c > r, r == c
    return (r <= c).astype(F32), c <= r, c < r, r == c


def _to_col(row, eye):
    return jnp.sum(jnp.where(eye, row, 0.0), axis=1, keepdims=True)


def _log_sigmoid(x):
    return jnp.minimum(x, 0.0) - jnp.log1p(jnp.exp(-jnp.abs(x)))


def _softplus(x):
    return jnp.maximum(x, 0.0) + jnp.log1p(jnp.exp(-jnp.abs(x)))


def _bdot(a, b):
    return jnp.dot(a.astype(BF16), b.astype(BF16), preferred_element_type=F32)


def _bdot_g(a, b, dims):
    return lax.dot_general(a.astype(BF16), b.astype(BF16), dims, preferred_element_type=F32)


def _ml_step(q, k, v, i_row, f_row, c_ref, n_ref, m_ref, reverse):
    incl, mask, _, eye = _chunk_masks(reverse)
    b_row = jnp.dot(f_row, incl, preferred_element_type=F32, precision=HI)
    b_col = _to_col(b_row, eye)
    gi_row = i_row - b_row
    m = m_ref[...]
    dlog = jnp.where(mask, b_col + gi_row, -jnp.inf)
    a_col = b_col + m
    mt = jnp.maximum(a_col, jnp.max(dlog, axis=1, keepdims=True))
    s = _bdot_g(q, k, NT_DIMS) * jnp.exp(dlog - mt)
    w_inter = jnp.exp(a_col - mt)
    cst = c_ref[...]
    nst = n_ref[...]
    num = _bdot(s, v) + w_inter * _bdot(q, cst)
    den = jnp.sum(s, axis=1, keepdims=True) + w_inter * jnp.sum(q * nst, axis=1, keepdims=True)
    h = num / jnp.maximum(jnp.abs(den), jnp.exp(-mt))
    b_last = b_row[:, 0:1] if reverse else b_row[:, CHUNK - 1:CHUNK]
    g_row = b_last + gi_row
    m_new = jnp.maximum(b_last + m, jnp.max(g_row, axis=1, keepdims=True))
    kw = k * _to_col(jnp.exp(g_row - m_new), eye)
    decay = jnp.exp(b_last + m - m_new)
    c_ref[...] = decay * cst + _bdot_g(kw, v, TN_DIMS)
    n_ref[...] = decay * nst + jnp.sum(kw, axis=0, keepdims=True)
    m_ref[...] = m_new
    return h


def _mlstm_body(ql_ref, kl_ref, vl_ref, ol_ref, qc_ref, kc_ref, vc_ref, oc_ref, gl_ref, gc_ref, gb_ref,
                cos_ref, sin_ref, ng_ref, *rest, need_ctx):
    if need_ctx:
        y_ref, yc_ref, c_scr, n_scr, m_scr, hl, hc = rest
    else:
        y_ref, c_scr, n_scr, m_scr, hl = rest
        yc_ref = hc = None
    c_scr[...] = jnp.zeros_like(c_scr)
    n_scr[...] = jnp.zeros_like(n_scr)
    m_scr[...] = jnp.zeros_like(m_scr)

    def run(q_ref, k_ref, v_ref, g_ref, h_scr, rope):
        nc = g_ref.shape[3]

        def body(j, carry):
            for d in range(2):
                jj = nc - 1 - j if d else j
                off = pl.multiple_of(jj * CHUNK, CHUNK)
                q = q_ref[0, pl.ds(off, CHUNK), :]
                k = k_ref[0, pl.ds(off, CHUNK), :]
                v = v_ref[0, pl.ds(off, CHUNK), :]
                if rope:
                    cs = cos_ref[pl.ds(off, CHUNK), :]
                    sn = sin_ref[pl.ds(off, CHUNK), :]
                    q = q * cs + pltpu.roll(q, ML_HEAD_DIM // 2, 1) * sn
                    k = k * cs + pltpu.roll(k, ML_HEAD_DIM // 2, 1) * sn
                q = q * (ML_HEAD_DIM ** -0.5)
                i_row = g_ref[0, 0, 2 * d, pl.ds(jj, 1), :] + gb_ref[0, 2 * d]
                f_row = _log_sigmoid(g_ref[0, 0, 2 * d + 1, pl.ds(jj, 1), :] + gb_ref[0, 2 * d + 1])
                h = _ml_step(q, k, v, i_row, f_row, c_scr.at[d], n_scr.at[d], m_scr.at[d], bool(d))
                if h_scr is not None:
                    h_scr[d, pl.ds(off, CHUNK), :] = h
            return carry

        lax.fori_loop(0, nc, body, 0)

    def finish(h_scr, o_ref, out_ref):
        h = h_scr[0] + h_scr[1]
        dev = h - jnp.mean(h, axis=-1, keepdims=True)
        var = jnp.mean(dev * dev, axis=-1, keepdims=True)
        out_ref[0] = dev * lax.rsqrt(var + EPS) * ng_ref[...] * _sigmoid(o_ref[0])

    run(qc_ref, kc_ref, vc_ref, gc_ref, hc, False)
    run(ql_ref, kl_ref, vl_ref, gl_ref, hl, True)
    finish(hl, ol_ref, y_ref)
    if need_ctx:
        finish(hc, oc_ref, yc_ref)


def _rope_tables(L):
    n_freq = ML_HEAD_DIM // 4
    t = jnp.arange(L)
    row = (t // GRID_W).astype(F32)
    col = (t % GRID_W).astype(F32)
    inv = ROPE_BASE ** (-jnp.arange(n_freq, dtype=F32) / n_freq)
    ang = jnp.concatenate([row[:, None] * inv, col[:, None] * inv], axis=-1)
    c, s = jnp.cos(ang), jnp.sin(ang)
    return jnp.concatenate([c, c], axis=-1), jnp.concatenate([-s, s], axis=-1)


def _gate_rows(g, n_types, heads):
    B, L, _ = g.shape
    return g.reshape(B, L, n_types, heads).transpose(0, 3, 2, 1).reshape(B, heads, n_types, L // CHUNK, CHUNK)


def _mlstm(p, pc, gate_b, norm_g, need_ctx):
    B, L, _ = p.shape
    Lc = pc.shape[1]
    H = ML_HEADS
    gl = _gate_rows(p[:, :, SMALL * LANE:SMALL * LANE + 4 * H], 4, H)
    gc = _gate_rows(pc[:, :, SMALL * LANE:SMALL * LANE + 4 * H], 4, H)
    gb = gate_b.T.reshape(H, 4, 1, 1)
    cos, sin = _rope_tables(L)
    lat = lambda col: pl.BlockSpec((1, L, LANE), lambda b, h: (b, 0, col + h))
    ctx = lambda col: pl.BlockSpec((1, Lc, LANE), lambda b, h: (b, 0, col + h))
    gspec = lambda n: pl.BlockSpec((1, 1, 4, n // CHUNK, CHUNK), lambda b, h: (b, h, 0, 0, 0))
    fix = lambda b, h: (0, 0)
    out_specs = [pl.BlockSpec((1, L, LANE), lambda b, h: (b, 0, h))]
    out_shape = [jax.ShapeDtypeStruct((B, L, GROUP_W), F32)]
    scratch = [pltpu.VMEM((2, ML_HEAD_DIM, ML_HEAD_DIM), F32), pltpu.VMEM((2, 1, ML_HEAD_DIM), F32),
               pltpu.VMEM((2, 1, 1), F32), pltpu.VMEM((2, L, LANE), F32)]
    if need_ctx:
        out_specs.append(pl.BlockSpec((1, Lc, LANE), lambda b, h: (b, 0, h)))
        out_shape.append(jax.ShapeDtypeStruct((B, Lc, GROUP_W), F32))
        scratch.append(pltpu.VMEM((2, Lc, LANE), F32))
    res = pl.pallas_call(
        functools.partial(_mlstm_body, need_ctx=need_ctx),
        grid=(B, H),
        in_specs=[lat(ML_Q), lat(ML_K), lat(ML_V), lat(ML_O), ctx(ML_Q), ctx(ML_K), ctx(ML_V), ctx(ML_O),
                  gspec(L), gspec(Lc), pl.BlockSpec((1, 4, 1, 1), lambda b, h: (h, 0, 0, 0)),
                  pl.BlockSpec((L, LANE), fix), pl.BlockSpec((L, LANE), fix),
                  pl.BlockSpec((1, LANE), lambda b, h: (0, h))],
        out_specs=out_specs,
        out_shape=out_shape,
        scratch_shapes=scratch,
        compiler_params=_params(("arbitrary", "arbitrary"), 40),
        name="mlstm",
    )(p, p, p, p, pc, pc, pc, pc, gl, gc, gb, cos, sin, norm_g[None, :])
    return (res[0], res[1]) if need_ctx else (res[0], None)


def _dn_step(q, k, v, a_row, beta_row, s_ref, reverse):
    incl, mask, strict, eye = _chunk_masks(reverse)
    g_row = jnp.dot(a_row, incl, preferred_element_type=F32, precision=HI)
    g_col = _to_col(g_row, eye)
    beta_col = _to_col(beta_row, eye)
    dec = jnp.exp(jnp.where(mask, g_col - g_row, -jnp.inf))
    kbeta = k * beta_col
    eg_col = jnp.exp(g_col)
    x = -jnp.where(strict, _bdot_g(kbeta, k, NT_DIMS) * dec, 0.0)
    inv = eye.astype(F32) + x
    for _ in range(5):
        x = jnp.dot(x, x, preferred_element_type=F32, precision=HI)
        inv = inv + jnp.dot(inv, x, preferred_element_type=F32, precision=HI)
    u = jnp.dot(inv, v * beta_col, preferred_element_type=F32, precision=HI)
    w = jnp.dot(inv, kbeta * eg_col, preferred_element_type=F32, precision=HI)
    st = s_ref[...]
    v_new = u - _bdot(w, st)
    attn = _bdot_g(q, k, NT_DIMS) * dec
    o = _bdot(q * eg_col, st) + _bdot(attn, v_new)
    g_last = g_row[:, 0:1] if reverse else g_row[:, CHUNK - 1:CHUNK]
    s_ref[...] = jnp.exp(g_last) * st + _bdot_g(k * jnp.exp(g_last - g_col), v_new, TN_DIMS)
    return o


def _deltanet_body(ql_ref, kl_ref, vl_ref, zl_ref, qc_ref, kc_ref, vc_ref, zc_ref, gl_ref, gc_ref, prm_ref,
                   ng_ref, *rest, need_ctx):
    if need_ctx:
        y_ref, yc_ref, s_scr, ol, oc = rest
    else:
        y_ref, s_scr, ol = rest
        yc_ref = oc = None
    s_scr[...] = jnp.zeros_like(s_scr)

    def unit(x):
        return x * lax.rsqrt(jnp.sum(x * x, axis=-1, keepdims=True) + EPS)

    def run(q_ref, k_ref, v_ref, g_ref, o_scr):
        nc = g_ref.shape[3]

        def body(j, carry):
            for d in range(2):
                jj = nc - 1 - j if d else j
                off = pl.multiple_of(jj * CHUNK, CHUNK)
                q = unit(q_ref[0, pl.ds(off, CHUNK), :]) * (DN_HEAD_DIM ** -0.5)
                k = unit(k_ref[0, pl.ds(off, CHUNK), :])
                v = v_ref[0, pl.ds(off, CHUNK), :]
                a_raw = g_ref[0, 0, 2 * d, pl.ds(jj, 1), :]
                a_row = -jnp.exp(prm_ref[0, 2 * d]) * _softplus(a_raw + prm_ref[0, 2 * d + 1])
                beta_row = _sigmoid(g_ref[0, 0, 2 * d + 1, pl.ds(jj, 1), :])
                o = _dn_step(q, k, v, a_row, beta_row, s_scr.at[d], bool(d))
                if o_scr is not None:
                    o_scr[d, pl.ds(off, CHUNK), :] = o
            return carry

        lax.fori_loop(0, nc, body, 0)

    def finish(o_scr, z_ref, out_ref):
        o = o_scr[0] + o_scr[1]
        z = z_ref[0]
        out_ref[0] = o * lax.rsqrt(_mean_sq(o) + EPS) * ng_ref[...] * (z * _sigmoid(z))

    run(qc_ref, kc_ref, vc_ref, gc_ref, oc)
    run(ql_ref, kl_ref, vl_ref, gl_ref, ol)
    finish(ol, zl_ref, y_ref)
    if need_ctx:
        finish(oc, zc_ref, yc_ref)


def _deltanet(p, pc, conv_w, a_log, dt_bias, norm_g, need_ctx):
    B, L, _ = p.shape
    Lc = pc.shape[1]
    H = DN_HEADS
    zero_b = jnp.zeros((1, 3 * GROUP_W), F32)
    u = _conv3(p, DN_Q, conv_w, zero_b, silu=True)
    uc = _conv3(pc, DN_Q, conv_w, zero_b, silu=True)

    def gates(src):
        base = SMALL * LANE + 4 * ML_HEADS
        beta = src[:, :, base:base + 2 * H].reshape(src.shape[0], src.shape[1], 2, 1, H)
        a = src[:, :, base + 2 * H:base + 4 * H].reshape(src.shape[0], src.shape[1], 2, 1, H)
        both = jnp.concatenate([a, beta], axis=3).reshape(src.shape[0], src.shape[1], 4 * H)
        return _gate_rows(both, 4, H)

    prm = jnp.stack([a_log[0], dt_bias[0], a_log[1], dt_bias[1]], axis=1).reshape(H, 4, 1, 1)
    lat = lambda blk: pl.BlockSpec((1, L, LANE), lambda b, h: (b, 0, blk + h))
    ctx = lambda blk: pl.BlockSpec((1, Lc, LANE), lambda b, h: (b, 0, blk + h))
    gspec = lambda n: pl.BlockSpec((1, 1, 4, n // CHUNK, CHUNK), lambda b, h: (b, h, 0, 0, 0))
    out_specs = [pl.BlockSpec((1, L, LANE), lambda b, h: (b, 0, h))]
    out_shape = [jax.ShapeDtypeStruct((B, L, GROUP_W), F32)]
    scratch = [pltpu.VMEM((2, DN_HEAD_DIM, DN_HEAD_DIM), F32), pltpu.VMEM((2, L, LANE), F32)]
    if need_ctx:
        out_specs.append(pl.BlockSpec((1, Lc, LANE), lambda b, h: (b, 0, h)))
        out_shape.append(jax.ShapeDtypeStruct((B, Lc, GROUP_W), F32))
        scratch.append(pltpu.VMEM((2, Lc, LANE), F32))
    res = pl.pallas_call(
        functools.partial(_deltanet_body, need_ctx=need_ctx),
        grid=(B, H),
        in_specs=[lat(0), lat(4), lat(8), lat(DN_G), ctx(0), ctx(4), ctx(8), ctx(DN_G),
                  gspec(L), gspec(Lc), pl.BlockSpec((1, 4, 1, 1), lambda b, h: (h, 0, 0, 0)),
                  pl.BlockSpec((1, LANE), lambda b, h: (0, 0))],
        out_specs=out_specs,
        out_shape=out_shape,
        scratch_shapes=scratch,
        compiler_params=_params(("arbitrary", "arbitrary"), 40),
        name="deltanet",
    )(u, u, u, p, uc, uc, uc, pc, gates(p), gates(pc), prm, norm_g[None, :])
    return (res[0], res[1]) if need_ctx else (res[0], None)


def _permute_w_in(w):
    D = w.shape[0]
    G = GROUP_W
    n_gate = 4 * ML_HEADS
    o = 0
    na, o = w[:, o:o + 3 * G], o + 3 * G
    ml_qkv, o = w[:, o:o + 3 * G], o + 3 * G
    ml_o, o = w[:, o:o + G], o + G
    ml_gates, o = w[:, o:o + n_gate], o + n_gate
    hy, o = w[:, o:o + 3 * G], o + 3 * G
    dn_qkv, o = w[:, o:o + 3 * G], o + 3 * G
    dn_gate, o = w[:, o:o + G], o + G
    dn_small = w[:, o:o + 4 * DN_HEADS]

    def split_pairs(m):
        return m.reshape(D, ML_HEADS, ML_HEAD_DIM // 2, 2).transpose(0, 1, 3, 2).reshape(D, G)

    used = 56 * LANE + n_gate + 4 * DN_HEADS
    parts = [na, split_pairs(ml_qkv[:, :G]), split_pairs(ml_qkv[:, G:2 * G]), ml_qkv[:, 2 * G:], ml_o, hy, dn_qkv,
             dn_gate, ml_gates, dn_small, jnp.zeros((D, P_COLS - used), w.dtype)]
    return jnp.concatenate(parts, axis=1).astype(BF16)


def kernel(x, c, ctx, c_ctx, w_ada, b_ada, norm_g, w_in, w_out, na_rpb, ml_gate_b, ml_norm_g, hy_conv_w, hy_conv_b, hy_f_w1, hy_f_b1, hy_f_w2, hy_f_b2, hy_f_w3, hy_f_b3, hy_skip, dn_conv_w, dn_a_log, dn_dt_bias, dn_norm_g, router_w, router_b, moe_w_gu, moe_b_gu, moe_w_down, moe_b_down):
    B, L, D = x.shape
    Lc = ctx.shape[1]
    assert D == D_MODEL and B < 16
    cc = jnp.zeros((16, D), F32).at[:B].set(c).at[B].set(c_ctx)
    mods = _adaln(cc, w_ada, b_ada)
    x2 = x.reshape(B * L, D)
    xc2 = ctx.reshape(B * Lc, D)
    flat = lambda y: y.reshape(-1, GROUP_W)
    for l in range(DEPTH):
        need_ctx = l < DEPTH - 1
        mod = mods[l, :B].reshape(B, 6, 1, D)
        mod_c = mods[l, B].reshape(1, 6, 1, D)
        g = norm_g[l]
        w_p = _permute_w_in(w_in[l])
        p = _inproj(x2, L, mod, g[0:1], w_p).reshape(B, L, P_COLS)
        pc = _inproj(xc2, Lc, mod_c, g[0:1], w_p).reshape(B, Lc, P_COLS)
        hy = (hy_conv_w[l], hy_conv_b[l], hy_f_w1[l], hy_f_b1[l], hy_f_w2[l], hy_f_b2[l], hy_f_w3[l], hy_f_b3[l],
              hy_skip[l])
        ya = _na_attention(p, pc, _na_bias_table(na_rpb[l]))
        yb, ybc = _mlstm(p, pc, ml_gate_b[l], ml_norm_g[l], need_ctx)
        yh = _hyena(p, *hy)
        yd, ydc = _deltanet(p, pc, dn_conv_w[l], dn_a_log[l], dn_dt_bias[l], dn_norm_g[l], need_ctx)
        w_out_bf = w_out[l].astype(BF16)
        rb = router_b[l][None, :]
        x2, tok, logits = _outproj([flat(ya), flat(yb), flat(yh), flat(yd)], w_out_bf, x2, L, mod, g[1:2], g[2:3],
                                   router_w[l], rb)
        if need_ctx:
            yac = _ctx_attention(pc)
            yhc = _hyena(pc, *hy)
            xc2, tok_c, logits_c = _outproj([flat(yac), flat(ybc), flat(yhc), flat(ydc)], w_out_bf, xc2, Lc, mod_c,
                                            g[1:2], g[2:3], router_w[l], rb)
            tok = jnp.concatenate([tok, tok_c], axis=0)
            logits = jnp.concatenate([logits, logits_c], axis=0)
        ps, idx3, rank3, gates, ys = _moe(tok, logits, moe_w_gu[l].astype(BF16), moe_b_gu[l],
                                          moe_w_down[l].astype(BF16), moe_b_down[l])
        nl = B * L // MOE_TM
        x2 = _combine(ps, idx3[:nl], rank3[:nl], gates[:B * L], ys, x2, L, mod, g[3:4])
        if need_ctx:
            xc2 = _combine(ps, idx3[nl:], rank3[nl:], gates[B * L:], ys, xc2, Lc, mod_c, g[3:4])
    return x2.reshape(B, L, D)
```

```python
import functools
import math

import numpy as np
import jax
import jax.numpy as jnp
from jax import lax
from jax.experimental import pallas as pl
from jax.experimental.pallas import tpu as pltpu

F32 = jnp.float32
BF16 = jnp.bfloat16
I32 = jnp.int32

D_MODEL = 2048
DEPTH = 2
GRID_W = 64
N_MIXERS = 4
GROUP_W = D_MODEL // N_MIXERS
NA_HEADS = 8
NA_HEAD_DIM = GROUP_W // NA_HEADS
NA_WIN_H = 8
NA_WIN_W = 16
ML_HEADS = 4
ML_HEAD_DIM = GROUP_W // ML_HEADS
HY_CH = GROUP_W
HY_BANDS = 8
HY_HIDDEN = 64
HY_SIN_FREQ = 1.0
HY_FAST_DECAY = 0.3
HY_SLOW_DECAY = 1.5
HY_DECAY_TARGET = 1e-2
DN_HEADS = 4
DN_HEAD_DIM = GROUP_W // DN_HEADS
CHUNK = 64
N_EXPERTS = 32
TOP_K = 4
D_FF = D_MODEL
SWIGLU_LIMIT = 7.0
SWIGLU_ALPHA = 1.702
MOE_BLOCK = 256
ROPE_BASE = 10000.0
EPS = 1e-6

LANE = 128
HALF = D_MODEL // 2
NEG = -1e30
HI = lax.Precision.HIGHEST

NA_Q, NA_K, NA_V = 0, 4, 8
ML_Q, ML_K, ML_V, ML_O = 12, 16, 20, 24
HY_U = 28
DN_Q = 40
DN_G = 52
SMALL = 56
P_COLS = 60 * LANE


def _params(sem, vmem_mb):
    return pltpu.CompilerParams(dimension_semantics=sem, vmem_limit_bytes=vmem_mb << 20)


def _mean_sq(x):
    return jnp.mean(x * x, axis=-1, keepdims=True)


def _sigmoid(x):
    return 1.0 / (1.0 + jnp.exp(-x))


def _adaln_body(c_ref, w_ref, b_ref, o_ref):
    c = c_ref[...]
    s = (c * _sigmoid(c)).astype(BF16)
    o_ref[0] = jnp.dot(s, w_ref[0].astype(BF16), preferred_element_type=F32) + b_ref[0]


def _adaln(cc, w_ada, b_ada):
    D = cc.shape[1]
    tn = 1024
    return pl.pallas_call(
        _adaln_body,
        grid=(DEPTH, 6 * D // tn),
        in_specs=[pl.BlockSpec((16, D), lambda l, j: (0, 0)),
                  pl.BlockSpec((1, D, tn), lambda l, j: (l, 0, j)),
                  pl.BlockSpec((1, 1, tn), lambda l, j: (l, 0, j))],
        out_specs=pl.BlockSpec((1, 16, tn), lambda l, j: (l, 0, j)),
        out_shape=jax.ShapeDtypeStruct((DEPTH, 16, 6 * D), F32),
        compiler_params=_params(("arbitrary", "arbitrary"), 40),
        name="adaln",
    )(cc, w_ada, b_ada.reshape(DEPTH, 1, 6 * D))


def _inproj_body(x_ref, g_ref, sc_ref, sh_ref, w_ref, o_ref, h_scr):
    @pl.when(pl.program_id(1) == 0)
    def _():
        x = x_ref[...]
        y = x * lax.rsqrt(_mean_sq(x) + EPS) * g_ref[...]
        h_scr[...] = (y * (1.0 + sc_ref[0, 0]) + sh_ref[0, 0]).astype(BF16)

    o_ref[...] = jnp.dot(h_scr[...], w_ref[...], preferred_element_type=F32)


def _inproj(x2, seq_len, mod, g, w_p):
    T, D = x2.shape
    tm = min(1024, seq_len)
    tn = 512
    per_seq = seq_len // tm
    bmap = (lambda i: i // per_seq) if mod.shape[0] > 1 else (lambda i: 0)
    return pl.pallas_call(
        _inproj_body,
        grid=(T // tm, P_COLS // tn),
        in_specs=[pl.BlockSpec((tm, D), lambda i, j: (i, 0)),
                  pl.BlockSpec((1, D), lambda i, j: (0, 0)),
                  pl.BlockSpec((1, 1, 1, D), lambda i, j: (bmap(i), 1, 0, 0)),
                  pl.BlockSpec((1, 1, 1, D), lambda i, j: (bmap(i), 0, 0, 0)),
                  pl.BlockSpec((D, tn), lambda i, j: (0, j))],
        out_specs=pl.BlockSpec((tm, tn), lambda i, j: (i, j)),
        out_shape=jax.ShapeDtypeStruct((T, P_COLS), F32),
        scratch_shapes=[pltpu.VMEM((tm, D), BF16)],
        compiler_params=_params(("arbitrary", "arbitrary"), 48),
        name="inproj",
    )(x2, g, mod, mod, w_p)


def _pack_halves(t):
    hi = pltpu.bitcast(t[:, :HALF].astype(BF16).astype(F32), I32)
    lo = pltpu.bitcast(t[:, HALF:].astype(BF16).astype(F32), I32)
    return hi | lax.shift_right_logical(lo, 16)


def _unpack_halves(u):
    hi = pltpu.bitcast(u & jnp.int32(-65536), F32)
    lo = pltpu.bitcast(lax.shift_left(u, 16), F32)
    return hi, lo


def _outproj_body(ya_ref, yb_ref, yh_ref, yd_ref, w_ref, x_ref, g1n_ref, gate_ref, g2n_ref, sc_ref, sh_ref,
                  rw_ref, rb_ref, xo_ref, tok_ref, lg_ref):
    G = GROUP_W
    y = jnp.dot(ya_ref[...].astype(BF16), w_ref[0:G, :], preferred_element_type=F32)
    y += jnp.dot(yb_ref[...].astype(BF16), w_ref[G:2 * G, :], preferred_element_type=F32)
    y += jnp.dot(yh_ref[...].astype(BF16), w_ref[2 * G:3 * G, :], preferred_element_type=F32)
    y += jnp.dot(yd_ref[...].astype(BF16), w_ref[3 * G:4 * G, :], preferred_element_type=F32)
    yn = y * lax.rsqrt(_mean_sq(y) + EPS) * g1n_ref[...]
    xn = x_ref[...] + gate_ref[0, 0] * yn
    xo_ref[...] = xn
    t = xn * lax.rsqrt(_mean_sq(xn) + EPS) * g2n_ref[...]
    t = t * (1.0 + sc_ref[0, 0]) + sh_ref[0, 0]
    tok_ref[...] = _pack_halves(t)
    lg_ref[...] = jnp.dot(t, rw_ref[...], preferred_element_type=F32, precision=HI) + rb_ref[...]


def _outproj(ys, w_out_bf, x2, seq_len, mod, g1n, g2n, router_w, router_b):
    T, D = x2.shape
    tm = 256
    per_seq = seq_len // tm
    bmap = (lambda i: i // per_seq) if mod.shape[0] > 1 else (lambda i: 0)
    row = lambda i: (i, 0)
    fix = lambda i: (0, 0)
    modspec = lambda k: pl.BlockSpec((1, 1, 1, D), lambda i: (bmap(i), k, 0, 0))
    return pl.pallas_call(
        _outproj_body,
        grid=(T // tm,),
        in_specs=[pl.BlockSpec((tm, GROUP_W), row)] * 4 + [
            pl.BlockSpec((D, D), fix), pl.BlockSpec((tm, D), row), pl.BlockSpec((1, D), fix), modspec(2),
            pl.BlockSpec((1, D), fix), modspec(4), modspec(3),
            pl.BlockSpec((D, N_EXPERTS), fix), pl.BlockSpec((1, N_EXPERTS), fix)],
        out_specs=[pl.BlockSpec((tm, D), row), pl.BlockSpec((tm, HALF), row), pl.BlockSpec((tm, N_EXPERTS), row)],
        out_shape=[jax.ShapeDtypeStruct((T, D), F32), jax.ShapeDtypeStruct((T, HALF), I32),
                   jax.ShapeDtypeStruct((T, N_EXPERTS), F32)],
        compiler_params=_params(("arbitrary",), 48),
        name="outproj",
    )(*ys, w_out_bf, x2, g1n, mod, g2n, mod, mod, router_w, router_b)


def _lane_select(cols, width, shape):
    lane = lax.broadcasted_iota(I32, shape, 1)
    out = jnp.broadcast_to(cols[width - 1], shape)
    for k in range(width - 2, -1, -1):
        out = jnp.where(lane == k, cols[k], out)
    return out


def _router_body(l_ref, idx_ref, rank_ref, gate_ref, cnt_ref, carry):
    @pl.when(pl.program_id(0) == 0)
    def _():
        carry[...] = jnp.zeros_like(carry)

    cur = l_ref[...]
    tm = cur.shape[0]
    lane = lax.broadcasted_iota(I32, cur.shape, 1)
    vals, ids, hots = [], [], []
    for _ in range(TOP_K):
        m = jnp.max(cur, axis=-1, keepdims=True)
        sel = jnp.min(jnp.where(cur == m, lane, N_EXPERTS), axis=-1, keepdims=True)
        hot = lane == sel
        vals.append(m)
        ids.append(sel)
        hots.append(hot)
        cur = jnp.where(hot, -jnp.inf, cur)
    es = [jnp.exp(v - vals[0]) for v in vals]
    tot = es[0] + es[1] + es[2] + es[3]
    cnt = sum(h.astype(F32) for h in hots)
    r = lax.broadcasted_iota(I32, (tm, tm), 0)
    c = lax.broadcasted_iota(I32, (tm, tm), 1)
    before = (c < r).astype(BF16)
    pre = jnp.dot(before, cnt.astype(BF16), preferred_element_type=F32) + carry[...]
    ranks = [jnp.sum(jnp.where(h, pre, 0.0), axis=-1, keepdims=True).astype(I32) for h in hots]
    carry[...] += jnp.sum(cnt, axis=0, keepdims=True)
    shape = (tm, TOP_K)
    idx_ref[...] = _lane_select(ids, TOP_K, shape)
    rank_ref[...] = _lane_select(ranks, TOP_K, shape)
    gate_ref[...] = _lane_select([e / tot for e in es], TOP_K, shape)
    cnt_ref[...] = carry[...].astype(I32)


def _router(logits):
    T = logits.shape[0]
    tm = 256
    row = lambda i: (i, 0)
    return pl.pallas_call(
        _router_body,
        grid=(T // tm,),
        in_specs=[pl.BlockSpec((tm, N_EXPERTS), row)],
        out_specs=[pl.BlockSpec((tm, TOP_K), row), pl.BlockSpec((tm, TOP_K), row), pl.BlockSpec((tm, TOP_K), row),
                   pl.BlockSpec((1, N_EXPERTS), lambda i: (0, 0))],
        out_shape=[jax.ShapeDtypeStruct((T, TOP_K), I32), jax.ShapeDtypeStruct((T, TOP_K), I32),
                   jax.ShapeDtypeStruct((T, TOP_K), F32), jax.ShapeDtypeStruct((1, N_EXPERTS), I32)],
        scratch_shapes=[pltpu.VMEM((1, N_EXPERTS), F32)],
        compiler_params=_params(("arbitrary",), 32),
        name="router",
    )(logits)


MOE_TM = 256


def _slot_copy(src_ref, dst_ref, src_row, dst_row, sem):
    return pltpu.make_async_copy(src_ref.at[pl.ds(src_row, 1)], dst_ref.at[pl.ds(dst_row, 1)], sem)


def _dispatch_body(ps_ref, idx_ref, rank_ref, tok_ref, xs_in_ref, xs_ref, sem):
    del xs_in_ref

    def start(r, carry):
        for k in range(TOP_K):
            d = ps_ref[idx_ref[0, 0, r * TOP_K + k]] + rank_ref[0, 0, r * TOP_K + k]
            _slot_copy(tok_ref, xs_ref, r, d, sem).start(priority=k % 2)
        return carry

    lax.fori_loop(0, MOE_TM, start, 0)

    def wait(r, carry):
        for k in range(TOP_K):
            _slot_copy(tok_ref, xs_ref, 0, 0, sem).wait()
        return carry

    lax.fori_loop(0, MOE_TM, wait, 0)


def _dispatch(pad_start, idx3, rank3, tok, n_slots):
    T = tok.shape[0]
    tm = MOE_TM
    smem = lambda: pl.BlockSpec((1, 1, tm * TOP_K), lambda i, ps: (i, 0, 0), memory_space=pltpu.SMEM)
    return pl.pallas_call(
        _dispatch_body,
        grid_spec=pltpu.PrefetchScalarGridSpec(
            num_scalar_prefetch=1,
            grid=(T // tm,),
            in_specs=[smem(), smem(), pl.BlockSpec((tm, HALF), lambda i, ps: (i, 0)),
                      pl.BlockSpec(memory_space=pl.ANY)],
            out_specs=pl.BlockSpec(memory_space=pl.ANY),
            scratch_shapes=[pltpu.SemaphoreType.DMA(())]),
        out_shape=jax.ShapeDtypeStruct((n_slots, HALF), I32),
        input_output_aliases={4: 0},
        compiler_params=_params(("arbitrary",), 32),
        name="moe_dispatch",
    )(pad_start, idx3, rank3, tok, jnp.zeros((n_slots, HALF), I32))


def _fresh_weights(be_ref, i):
    return jnp.logical_or(i == 0, be_ref[i] != be_ref[jnp.maximum(i - 1, 0)])


def _g1_body(be_ref, nu_ref, x_ref, wg_ref, wu_ref, bg_ref, bu_ref, h_ref, wg_bf, wu_bf):
    i = pl.program_id(1)
    used = i < nu_ref[0]

    @pl.when(jnp.logical_and(used, _fresh_weights(be_ref, i)))
    def _():
        wg_bf[...] = wg_ref[0, 0].astype(BF16)
        wu_bf[...] = wu_ref[0, 0].astype(BF16)

    @pl.when(used)
    def _():
        hi, lo = _unpack_halves(x_ref[...])
        hi = hi.astype(BF16)
        lo = lo.astype(BF16)

        def proj(w_bf, b_ref):
            return (jnp.dot(hi, w_bf[:HALF, :], preferred_element_type=F32)
                    + jnp.dot(lo, w_bf[HALF:, :], preferred_element_type=F32) + b_ref[0, 0])

        g = jnp.minimum(proj(wg_bf, bg_ref), SWIGLU_LIMIT)
        u = jnp.clip(proj(wu_bf, bu_ref), -SWIGLU_LIMIT, SWIGLU_LIMIT)
        h_ref[...] = (g * _sigmoid(SWIGLU_ALPHA * g) * (u + 1.0)).astype(BF16)

    @pl.when(jnp.logical_not(used))
    def _():
        h_ref[...] = jnp.zeros_like(h_ref)


def _expert_up(block_expert, n_used, xs, w_gu, b_gu4, layer):
    n_slots = xs.shape[0]
    nb = n_slots // MOE_BLOCK
    tn = 1024
    nj = D_FF // tn
    last = lambda i, nu: jnp.minimum(i, nu[0] - 1)
    wspec = lambda off: pl.BlockSpec((1, 1, D_MODEL, tn), lambda j, i, be, nu: (layer, be[last(i, nu)], 0, off + j))
    bspec = lambda off: pl.BlockSpec((1, 1, 1, tn), lambda j, i, be, nu: (layer, be[last(i, nu)], 0, off + j))
    return pl.pallas_call(
        _g1_body,
        grid_spec=pltpu.PrefetchScalarGridSpec(
            num_scalar_prefetch=2,
            grid=(nj, nb),
            in_specs=[pl.BlockSpec((MOE_BLOCK, HALF), lambda j, i, be, nu: (last(i, nu), 0)),
                      wspec(0), wspec(nj), bspec(0), bspec(nj)],
            out_specs=pl.BlockSpec((MOE_BLOCK, tn), lambda j, i, be, nu: (i, j)),
            scratch_shapes=[pltpu.VMEM((D_MODEL, tn), BF16), pltpu.VMEM((D_MODEL, tn), BF16)]),
        out_shape=jax.ShapeDtypeStruct((n_slots, D_FF), BF16),
        compiler_params=_params(("arbitrary", "arbitrary"), 56),
        name="moe_up",
    )(block_expert, n_used, xs, w_gu, w_gu, b_gu4, b_gu4)


def _g2_body(be_ref, nu_ref, h_ref, w_ref, b_ref, y_ref, w_bf):
    i = pl.program_id(0)
    used = i < nu_ref[0]

    @pl.when(jnp.logical_and(used, _fresh_weights(be_ref, i)))
    def _():
        w_bf[...] = w_ref[0, 0].astype(BF16)

    @pl.when(used)
    def _():
        y_ref[...] = _pack_halves(jnp.dot(h_ref[...], w_bf[...], preferred_element_type=F32) + b_ref[0, 0])

    @pl.when(jnp.logical_not(used))
    def _():
        y_ref[...] = jnp.zeros_like(y_ref)


def _expert_down(block_expert, n_used, h, w_down, b_down4, layer):
    n_slots = h.shape[0]
    nb = n_slots // MOE_BLOCK
    last = lambda i, nu: jnp.minimum(i, nu[0] - 1)
    return pl.pallas_call(
        _g2_body,
        grid_spec=pltpu.PrefetchScalarGridSpec(
            num_scalar_prefetch=2,
            grid=(nb,),
            in_specs=[pl.BlockSpec((MOE_BLOCK, D_FF), lambda i, be, nu: (last(i, nu), 0)),
                      pl.BlockSpec((1, 1, D_FF, D_MODEL), lambda i, be, nu: (layer, be[last(i, nu)], 0, 0)),
                      pl.BlockSpec((1, 1, 1, D_MODEL), lambda i, be, nu: (layer, be[last(i, nu)], 0, 0))],
            out_specs=pl.BlockSpec((MOE_BLOCK, HALF), lambda i, be, nu: (i, 0)),
            scratch_shapes=[pltpu.VMEM((D_FF, D_MODEL), BF16)]),
        out_shape=jax.ShapeDtypeStruct((n_slots, HALF), I32),
        compiler_params=_params(("arbitrary",), 56),
        name="moe_down",
    )(block_expert, n_used, h, w_down, b_down4)


def _combine_body(ps_ref, idx_ref, rank_ref, gate_ref, ys_ref, x_ref, gn_ref, g2_ref, o_ref, buf, sem):
    def start(r, carry):
        for k in range(TOP_K):
            d = ps_ref[idx_ref[0, 0, r * TOP_K + k]] + rank_ref[0, 0, r * TOP_K + k]
            _slot_copy(ys_ref, buf.at[k], d, r, sem).start(priority=k % 2)
        return carry

    lax.fori_loop(0, MOE_TM, start, 0)

    def wait(r, carry):
        for k in range(TOP_K):
            _slot_copy(ys_ref, buf.at[k], 0, 0, sem).wait()
        return carry

    lax.fori_loop(0, MOE_TM, wait, 0)

    gate = gate_ref[...]
    f_hi = f_lo = None
    for k in range(TOP_K):
        hi, lo = _unpack_halves(buf[k])
        gk = gate[:, k:k + 1]
        f_hi = gk * hi if k == 0 else f_hi + gk * hi
        f_lo = gk * lo if k == 0 else f_lo + gk * lo
    ms = (jnp.sum(f_hi * f_hi, axis=-1, keepdims=True) + jnp.sum(f_lo * f_lo, axis=-1, keepdims=True)) / D_MODEL
    scale = lax.rsqrt(ms + EPS)
    gn = gn_ref[...]
    g2 = g2_ref[0, 0]
    o_ref[:, :HALF] = x_ref[:, :HALF] + g2[:, :HALF] * (f_hi * scale * gn[:, :HALF])
    o_ref[:, HALF:] = x_ref[:, HALF:] + g2[:, HALF:] * (f_lo * scale * gn[:, HALF:])


def _combine(pad_start, idx3, rank3, gates, ys, x2, seq_len, mod, gn):
    T, D = x2.shape
    tm = MOE_TM
    per_seq = seq_len // tm
    bmap = (lambda i: i // per_seq) if mod.shape[0] > 1 else (lambda i: 0)
    smem = lambda: pl.BlockSpec((1, 1, tm * TOP_K), lambda i, ps: (i, 0, 0), memory_space=pltpu.SMEM)
    return pl.pallas_call(
        _combine_body,
        grid_spec=pltpu.PrefetchScalarGridSpec(
            num_scalar_prefetch=1,
            grid=(T // tm,),
            in_specs=[smem(), smem(), pl.BlockSpec((tm, TOP_K), lambda i, ps: (i, 0)),
                      pl.BlockSpec(memory_space=pl.ANY),
                      pl.BlockSpec((tm, D), lambda i, ps: (i, 0)),
                      pl.BlockSpec((1, D), lambda i, ps: (0, 0)),
                      pl.BlockSpec((1, 1, 1, D), lambda i, ps: (bmap(i), 5, 0, 0))],
            out_specs=pl.BlockSpec((tm, D), lambda i, ps: (i, 0)),
            scratch_shapes=[pltpu.VMEM((TOP_K, tm, HALF), I32), pltpu.SemaphoreType.DMA(())]),
        out_shape=jax.ShapeDtypeStruct((T, D), F32),
        compiler_params=_params(("arbitrary",), 40),
        name="moe_combine",
    )(pad_start, idx3, rank3, gates, ys, x2, gn, mod)


def _moe(tok, logits, w_gu, b_gu, w_down, b_down, layer):
    T = tok.shape[0]
    idx, rank, gates, counts = _router(logits)
    counts = counts[0]
    padded = (counts + MOE_BLOCK - 1) // MOE_BLOCK * MOE_BLOCK
    pad_end = jnp.cumsum(padded)
    pad_start = (pad_end - padded).astype(I32)
    n_slots = (T * TOP_K // MOE_BLOCK + N_EXPERTS) * MOE_BLOCK
    nb = n_slots // MOE_BLOCK
    blk = jnp.arange(nb, dtype=I32) * MOE_BLOCK
    block_expert = jnp.minimum(jnp.sum((blk[:, None] >= pad_end[None, :]).astype(I32), axis=1), N_EXPERTS - 1)
    n_used = (pad_end[-1:] // MOE_BLOCK).astype(I32)
    idx3 = idx.reshape(T // MOE_TM, 1, MOE_TM * TOP_K)
    rank3 = rank.reshape(T // MOE_TM, 1, MOE_TM * TOP_K)
    xs = _dispatch(pad_start, idx3, rank3, tok, n_slots)
    h = _expert_up(block_expert, n_used, xs, w_gu, b_gu.reshape(DEPTH, N_EXPERTS, 1, 2 * D_FF), layer)
    ys = _expert_down(block_expert, n_used, h, w_down, b_down.reshape(DEPTH, N_EXPERTS, 1, D_MODEL), layer)
    return pad_start, idx3, rank3, gates, ys


NA_QT = 4
NA_BAND = 12
NA_ROWS = 32


def _na_bias_table(rpb):
    W = GRID_W
    cols = np.arange(W)
    cs = np.clip(cols - NA_WIN_W // 2, 0, W - NA_WIN_W)
    valid_c = (cols[None, :] >= cs[:, None]) & (cols[None, :] < cs[:, None] + NA_WIN_W)
    dc = np.clip(cols[None, :] - cols[:, None] + NA_WIN_W - 1, 0, 2 * NA_WIN_W - 2)
    tm = jnp.where(valid_c, rpb[:, :, dc], NEG)
    nt = NA_ROWS // NA_QT
    qt = np.arange(nt)
    r = NA_QT * qt[:, None] + np.arange(NA_QT)[None, :]
    kb = np.clip(NA_QT * qt - NA_WIN_H // 2, 0, NA_ROWS - NA_BAND)
    kr = kb[:, None] + np.arange(NA_BAND)[None, :]
    rs = np.clip(r - NA_WIN_H // 2, 0, NA_ROWS - NA_WIN_H)
    valid_r = (kr[:, None, :] >= rs[:, :, None]) & (kr[:, None, :] < rs[:, :, None] + NA_WIN_H)
    dr = np.clip(kr[:, None, :] - r[:, :, None] + NA_WIN_H - 1, 0, 2 * NA_WIN_H - 2)
    blocks = jnp.where(valid_r[None, :, :, :, None, None], tm[:, dr], NEG)
    return blocks.transpose(1, 0, 2, 4, 3, 5).reshape(nt, NA_HEADS, NA_QT * W, NA_BAND * W)


def _masked_heads_attention(q, keys, values, biases):
    lane = lax.broadcasted_iota(I32, q.shape, 1)
    nt = (((1,), (1,)), ((), ()))
    out = None
    for hh in range(2):
        hm = (lane >= NA_HEAD_DIM) == bool(hh)
        qh = jnp.where(hm, q, 0.0).astype(BF16)
        ss = []
        for kk, bb in zip(keys, biases):
            s = lax.dot_general(qh, kk, nt, preferred_element_type=F32)
            ss.append(s if bb is None else s + bb[hh])
        m = ss[0].max(axis=1, keepdims=True)
        for s in ss[1:]:
            m = jnp.maximum(m, s.max(axis=1, keepdims=True))
        den = None
        o = None
        for s, vv in zip(ss, values):
            p = jnp.exp(s - m)
            d = jnp.sum(p, axis=1, keepdims=True)
            den = d if den is None else den + d
            pv = jnp.dot(p.astype(BF16), vv, preferred_element_type=F32)
            o = pv if o is None else o + pv
        o = o / den
        out = o if hh == 0 else jnp.where(hm, o, out)
    return out


def _na_body(q_ref, k_ref, v_ref, kc_ref, vc_ref, b_ref, o_ref):
    qt = pl.program_id(2)
    kb = pl.multiple_of(jnp.clip(qt * NA_QT - NA_WIN_H // 2, 0, NA_ROWS - NA_BAND) * GRID_W, GRID_W)
    band = NA_BAND * GRID_W
    q = q_ref[0] * (NA_HEAD_DIM ** -0.5)
    kl = k_ref[0, pl.ds(kb, band), :].astype(BF16)
    vl = v_ref[0, pl.ds(kb, band), :].astype(BF16)
    o_ref[0] = _masked_heads_attention(q, [kl, kc_ref[0].astype(BF16)], [vl, vc_ref[0].astype(BF16)],
                                       [(b_ref[0, 0], b_ref[0, 1]), None])


def _na_attention(p, pc, bias):
    B, L, _ = p.shape
    Lc = pc.shape[1]
    assert L == NA_ROWS * GRID_W
    tq = NA_QT * GRID_W
    return pl.pallas_call(
        _na_body,
        grid=(B, NA_HEADS // 2, L // tq),
        in_specs=[pl.BlockSpec((1, tq, LANE), lambda b, h, t: (b, t, NA_Q + h)),
                  pl.BlockSpec((1, L, LANE), lambda b, h, t: (b, 0, NA_K + h)),
                  pl.BlockSpec((1, L, LANE), lambda b, h, t: (b, 0, NA_V + h)),
                  pl.BlockSpec((1, Lc, LANE), lambda b, h, t: (b, 0, NA_K + h)),
                  pl.BlockSpec((1, Lc, LANE), lambda b, h, t: (b, 0, NA_V + h)),
                  pl.BlockSpec((1, 2, tq, NA_BAND * GRID_W), lambda b, h, t: (t, h, 0, 0))],
        out_specs=pl.BlockSpec((1, tq, LANE), lambda b, h, t: (b, t, h)),
        out_shape=jax.ShapeDtypeStruct((B, L, GROUP_W), F32),
        compiler_params=_params(("arbitrary", "arbitrary", "arbitrary"), 40),
        name="na_attention",
    )(p, p, p, pc, pc, bias)


def _ctx_attn_body(q_ref, k_ref, v_ref, o_ref):
    q = q_ref[0] * (NA_HEAD_DIM ** -0.5)
    o_ref[0] = _masked_heads_attention(q, [k_ref[0].astype(BF16)], [v_ref[0].astype(BF16)], [None])


def _ctx_attention(pc):
    B, Lc, _ = pc.shape
    return pl.pallas_call(
        _ctx_attn_body,
        grid=(B, NA_HEADS // 2),
        in_specs=[pl.BlockSpec((1, Lc, LANE), lambda b, h: (b, 0, NA_Q + h)),
                  pl.BlockSpec((1, Lc, LANE), lambda b, h: (b, 0, NA_K + h)),
                  pl.BlockSpec((1, Lc, LANE), lambda b, h: (b, 0, NA_V + h))],
        out_specs=pl.BlockSpec((1, Lc, LANE), lambda b, h: (b, 0, h)),
        out_shape=jax.ShapeDtypeStruct((B, Lc, GROUP_W), F32),
        compiler_params=_params(("arbitrary", "arbitrary"), 32),
        name="ctx_attention",
    )(pc, pc, pc)


CONV_TT = 256
HALO = 8


def _conv3_body(xp_ref, x_ref, xn_ref, w_ref, b_ref, o_ref, *, silu, nt):
    i = pl.program_id(1)
    x = x_ref[0]
    tt = x.shape[0]
    prev_row = jnp.where(i == 0, 0.0, xp_ref[0, HALO - 1:HALO, :])
    next_row = jnp.where(i == nt - 1, 0.0, xn_ref[0, 0:1, :])
    row = lax.broadcasted_iota(I32, x.shape, 0)
    xm = jnp.where(row == 0, prev_row, pltpu.roll(x, 1, 0))
    xp = jnp.where(row == tt - 1, next_row, pltpu.roll(x, tt - 1, 0))
    y = xm * w_ref[0:1, :] + x * w_ref[1:2, :] + xp * w_ref[2:3, :] + b_ref[...]
    o_ref[0] = y * _sigmoid(y) if silu else y


def _conv3(p, col0, w, b, silu):
    B, L, _ = p.shape
    C = w.shape[1]
    cw = 512
    tt = min(CONV_TT, L)
    nt = L // tt
    hb = tt // HALO
    cb = col0 * LANE // cw
    assert col0 * LANE % cw == 0
    return pl.pallas_call(
        functools.partial(_conv3_body, silu=silu, nt=nt),
        grid=(B, nt, C // cw),
        in_specs=[pl.BlockSpec((1, HALO, cw), lambda bb, i, j: (bb, jnp.maximum(i * hb - 1, 0), cb + j)),
                  pl.BlockSpec((1, tt, cw), lambda bb, i, j: (bb, i, cb + j)),
                  pl.BlockSpec((1, HALO, cw), lambda bb, i, j: (bb, jnp.minimum((i + 1) * hb, L // HALO - 1), cb + j)),
                  pl.BlockSpec((3, cw), lambda bb, i, j: (0, j)),
                  pl.BlockSpec((1, cw), lambda bb, i, j: (0, j))],
        out_specs=pl.BlockSpec((1, tt, cw), lambda bb, i, j: (bb, i, j)),
        out_shape=jax.ShapeDtypeStruct((B, L, C), F32),
        compiler_params=_params(("arbitrary", "arbitrary", "arbitrary"), 32),
        name="conv3",
    )(p, p, p, w, b)


def _dft_tables(L):
    N = 2 * L
    f = jnp.arange(L, dtype=I32)[:, None]
    t = jnp.arange(L, dtype=I32)[None, :]
    ang = ((f * t) % N).astype(F32) * (2.0 * math.pi / N)
    c, s = jnp.cos(ang), jnp.sin(ang)
    nyq = jnp.where(t % 2 == 0, 1.0, -1.0).astype(F32)
    dc = f == 0
    fc = c
    fs = jnp.where(dc, nyq, -s)
    ci = (jnp.where(dc, 1.0, 2.0) * c / N).T
    si = (jnp.where(dc, nyq, -2.0 * s) / N).T
    return fc.astype(BF16), fs.astype(BF16), ci.astype(BF16), si.astype(BF16)


def _hy_filter_body(w1_ref, b1_ref, w2_ref, b2_ref, w3_ref, b3_ref, win_ref, o_ref, *, L, tt):
    i = pl.program_id(0)
    pos = (lax.broadcasted_iota(I32, (tt, LANE), 0) + i * tt).astype(F32) / L
    lane = lax.broadcasted_iota(I32, (tt, LANE), 1)
    band = jnp.where(lane <= HY_BANDS, lane, lane - HY_BANDS).astype(F32)
    ang = (2.0 * math.pi) * pos * band
    feats = jnp.where(lane == 0, pos, jnp.where(lane <= HY_BANDS, jnp.sin(ang),
                                                jnp.where(lane <= 2 * HY_BANDS, jnp.cos(ang), 0.0)))
    h = jnp.sin(HY_SIN_FREQ * (jnp.dot(feats, w1_ref[...], preferred_element_type=F32, precision=HI) + b1_ref[...]))
    h = jnp.sin(HY_SIN_FREQ * (jnp.dot(h, w2_ref[...], preferred_element_type=F32, precision=HI) + b2_ref[...]))
    h = jnp.dot(h, w3_ref[...], preferred_element_type=F32, precision=HI) + b3_ref[...]
    t1 = (lax.broadcasted_iota(I32, (tt, 1), 0) + i * tt).astype(F32) / L
    h = h * jnp.exp(-t1 * win_ref[...])
    row = lax.broadcasted_iota(I32, h.shape, 0) + i * tt
    col = lax.broadcasted_iota(I32, h.shape, 1)
    bwd = (col // HY_CH) % 2 == 1
    o_ref[...] = jnp.where(jnp.logical_and(row == 0, bwd), 0.0, h)


def _hy_filters(L, w1, b1, w2, b2, w3, b3):
    tt = min(256, L)
    nc = 4 * HY_CH
    w1p = jnp.zeros((LANE, HY_HIDDEN), F32).at[:w1.shape[0]].set(w1)
    max_decay = math.log(HY_DECAY_TARGET) / HY_FAST_DECAY
    min_decay = math.log(HY_DECAY_TARGET) / HY_SLOW_DECAY
    deltas = jnp.abs(jnp.linspace(min_decay, max_decay, HY_CH, dtype=F32))
    win = jnp.tile(deltas, 4)[None, :]
    fix = lambda i: (0, 0)
    return pl.pallas_call(
        functools.partial(_hy_filter_body, L=L, tt=tt),
        grid=(L // tt,),
        in_specs=[pl.BlockSpec((LANE, HY_HIDDEN), fix), pl.BlockSpec((1, HY_HIDDEN), fix),
                  pl.BlockSpec((HY_HIDDEN, HY_HIDDEN), fix), pl.BlockSpec((1, HY_HIDDEN), fix),
                  pl.BlockSpec((HY_HIDDEN, nc), fix), pl.BlockSpec((1, nc), fix), pl.BlockSpec((1, nc), fix)],
        out_specs=pl.BlockSpec((tt, nc), lambda i: (i, 0)),
        out_shape=jax.ShapeDtypeStruct((L, nc), F32),
        compiler_params=_params(("arbitrary",), 32),
        name="hyena_filters",
    )(w1p, b1[None, :], w2, b2[None, :], w3, b3[None, :], win)


def _hy_spectrum_body(fc_ref, fs_ref, h_ref, gc_ref, gs_ref):
    hb = h_ref[...].astype(BF16)
    gc_ref[...] = jnp.dot(fc_ref[...], hb, preferred_element_type=F32)
    gs_ref[...] = jnp.dot(fs_ref[...], hb, preferred_element_type=F32)


def _hy_spectrum(fc, fs, hmat):
    L, nc = hmat.shape
    tf = min(512, L)
    tn = 512
    return pl.pallas_call(
        _hy_spectrum_body,
        grid=(nc // tn, L // tf),
        in_specs=[pl.BlockSpec((tf, L), lambda j, i: (i, 0)), pl.BlockSpec((tf, L), lambda j, i: (i, 0)),
                  pl.BlockSpec((L, tn), lambda j, i: (0, j))],
        out_specs=[pl.BlockSpec((tf, tn), lambda j, i: (i, j)), pl.BlockSpec((tf, tn), lambda j, i: (i, j))],
        out_shape=[jax.ShapeDtypeStruct((L, nc), F32), jax.ShapeDtypeStruct((L, nc), F32)],
        compiler_params=_params(("arbitrary", "arbitrary"), 40),
        name="hyena_spectrum",
    )(fc, fs, hmat)


def _hy_fwd_body(fc_ref, fs_ref, z_ref, gc_ref, gs_ref, yre_ref, yim_ref, zb):
    i = pl.program_id(1)

    @pl.when(i == 0)
    def _():
        zb[...] = z_ref[0].astype(BF16)

    C = HY_CH
    zre = jnp.dot(fc_ref[...], zb[...], preferred_element_type=F32)
    zim = jnp.dot(fs_ref[...], zb[...], preferred_element_type=F32)
    gre = gc_ref[:, :C] + gc_ref[:, C:]
    a = gs_ref[:, :C]
    b = gs_ref[:, C:]
    packed = jnp.logical_and(lax.broadcasted_iota(I32, zre.shape, 0) == 0, i == 0)
    gim = jnp.where(packed, a + b, a - b)
    yre_ref[0] = jnp.where(packed, zre * gre, zre * gre - zim * gim).astype(BF16)
    yim_ref[0] = jnp.where(packed, zim * gim, zre * gim + zim * gre).astype(BF16)


def _hy_forward(fc, fs, z_src, z_col, gc, gs, order):
    B, L, _ = z_src.shape
    C = HY_CH
    tf = min(512, L)
    return pl.pallas_call(
        _hy_fwd_body,
        grid=(B, L // tf),
        in_specs=[pl.BlockSpec((tf, L), lambda b, i: (i, 0)), pl.BlockSpec((tf, L), lambda b, i: (i, 0)),
                  pl.BlockSpec((1, L, C), lambda b, i: (b, 0, z_col)),
                  pl.BlockSpec((tf, 2 * C), lambda b, i: (i, order)), pl.BlockSpec((tf, 2 * C), lambda b, i: (i, order))],
        out_specs=[pl.BlockSpec((1, tf, C), lambda b, i: (b, i, 0)), pl.BlockSpec((1, tf, C), lambda b, i: (b, i, 0))],
        out_shape=[jax.ShapeDtypeStruct((B, L, C), BF16), jax.ShapeDtypeStruct((B, L, C), BF16)],
        scratch_shapes=[pltpu.VMEM((L, C), BF16)],
        compiler_params=_params(("arbitrary", "arbitrary"), 40),
        name="hyena_forward",
    )(fc, fs, z_src, gc, gs)


def _hy_inv_body(ci_ref, si_ref, yre_ref, yim_ref, z_ref, x_ref, skip_ref, o_ref):
    y = jnp.dot(ci_ref[...], yre_ref[0], preferred_element_type=F32)
    y += jnp.dot(si_ref[...], yim_ref[0], preferred_element_type=F32)
    o_ref[0] = x_ref[0] * (y + z_ref[0] * skip_ref[0])


def _hy_inverse(ci, si, yre, yim, z_src, z_col, x_src, x_col, skip, order):
    B, L, C = yre.shape
    tt = min(512, L)
    return pl.pallas_call(
        _hy_inv_body,
        grid=(B, L // tt),
        in_specs=[pl.BlockSpec((tt, L), lambda b, i: (i, 0)), pl.BlockSpec((tt, L), lambda b, i: (i, 0)),
                  pl.BlockSpec((1, L, C), lambda b, i: (b, 0, 0)), pl.BlockSpec((1, L, C), lambda b, i: (b, 0, 0)),
                  pl.BlockSpec((1, tt, C), lambda b, i: (b, i, z_col)),
                  pl.BlockSpec((1, tt, C), lambda b, i: (b, i, x_col)),
                  pl.BlockSpec((1, 1, C), lambda b, i: (order, 0, 0))],
        out_specs=pl.BlockSpec((1, tt, C), lambda b, i: (b, i, 0)),
        out_shape=jax.ShapeDtypeStruct((B, L, C), F32),
        compiler_params=_params(("arbitrary", "arbitrary"), 40),
        name="hyena_inverse",
    )(ci, si, yre, yim, z_src, x_src, skip)


def _hyena(p, conv_w, conv_b, w1, b1, w2, b2, w3, b3, skip):
    L = p.shape[1]
    u = _conv3(p, HY_U, conv_w, conv_b[None, :], silu=False)
    fc, fs, ci, si = _dft_tables(L)
    gc, gs = _hy_spectrum(fc, fs, _hy_filters(L, w1, b1, w2, b2, w3, b3))
    skip3 = skip[:, None, :]
    yre, yim = _hy_forward(fc, fs, u, 2, gc, gs, 0)
    z1 = _hy_inverse(ci, si, yre, yim, u, 2, u, 0, skip3, 0)
    yre, yim = _hy_forward(fc, fs, z1, 0, gc, gs, 1)
    return _hy_inverse(ci, si, yre, yim, z1, 0, u, 1, skip3, 1)


REC_HP = 2
REC_UNROLL = 2
NT_DIMS = (((1,), (1,)), ((), ()))
TN_DIMS = (((0,), (0,)), ((), ()))


def _chunk_masks(reverse):
    r = lax.broadcasted_iota(I32, (CHUNK, CHUNK), 0)
    c = lax.broadcasted_iota(I32, (CHUNK, CHUNK), 1)
    if reverse:
        return (r >= c).astype(F32), c >= r, c > r, r == c
    return (r <= c).astype(F32), c <= r, c < r, r == c


def _to_col(row, eye):
    return jnp.sum(jnp.where(eye, row, 0.0), axis=1, keepdims=True)


def _log_sigmoid(x):
    return jnp.minimum(x, 0.0) - jnp.log1p(jnp.exp(-jnp.abs(x)))


def _softplus(x):
    return jnp.maximum(x, 0.0) + jnp.log1p(jnp.exp(-jnp.abs(x)))


def _bdot(a, b):
    return jnp.dot(a.astype(BF16), b.astype(BF16), preferred_element_type=F32)


def _split_bf16(a):
    hi = a.astype(BF16)
    return hi, (a - hi.astype(F32)).astype(BF16)


def _dot3(a, b):
    d = lambda x, y: jnp.dot(x, y, preferred_element_type=F32)
    return d(a[0], b[0]) + (d(a[0], b[1]) + d(a[1], b[0]))


def _bdot_g(a, b, dims):
    return lax.dot_general(a.astype(BF16), b.astype(BF16), dims, preferred_element_type=F32)


def _ml_steps(items, c_refs, n_refs, m_refs):
    n = range(len(items))
    q, k, v, i_row, f_row, rev = zip(*items)
    masks = [_chunk_masks(r) for r in rev]
    incl, mask, _, eye = zip(*masks)
    b_row = [jnp.dot(f_row[i], incl[i], preferred_element_type=F32, precision=HI) for i in n]
    qk = [_bdot_g(q[i], k[i], NT_DIMS) for i in n]
    cst = [c_refs[i][...] for i in n]
    nst = [n_refs[i][...] for i in n]
    m = [m_refs[i][...] for i in n]
    qc = [_bdot(q[i], cst[i]) for i in n]
    b_col = [_to_col(b_row[i], eye[i]) for i in n]
    gi_row = [i_row[i] - b_row[i] for i in n]
    dlog = [jnp.where(mask[i], b_col[i] + gi_row[i], -jnp.inf) for i in n]
    a_col = [b_col[i] + m[i] for i in n]
    mt = [jnp.maximum(a_col[i], jnp.max(dlog[i], axis=1, keepdims=True)) for i in n]
    s = [qk[i] * jnp.exp(dlog[i] - mt[i]) for i in n]
    w_inter = [jnp.exp(a_col[i] - mt[i]) for i in n]
    sv = [_bdot(s[i], v[i]) for i in n]
    b_last = [b_row[i][:, 0:1] if rev[i] else b_row[i][:, CHUNK - 1:CHUNK] for i in n]
    g_row = [b_last[i] + gi_row[i] for i in n]
    m_new = [jnp.maximum(b_last[i] + m[i], jnp.max(g_row[i], axis=1, keepdims=True)) for i in n]
    kw = [k[i] * _to_col(jnp.exp(g_row[i] - m_new[i]), eye[i]) for i in n]
    kv = [_bdot_g(kw[i], v[i], TN_DIMS) for i in n]
    decay = [jnp.exp(b_last[i] + m[i] - m_new[i]) for i in n]
    hs = []
    for i in n:
        num = sv[i] + w_inter[i] * qc[i]
        den = jnp.sum(s[i], axis=1, keepdims=True) + w_inter[i] * jnp.sum(q[i] * nst[i], axis=1, keepdims=True)
        hs.append(num / jnp.maximum(jnp.abs(den), jnp.exp(-mt[i])))
        c_refs[i][...] = decay[i] * cst[i] + kv[i]
        n_refs[i][...] = decay[i] * nst[i] + jnp.sum(kw[i], axis=0, keepdims=True)
        m_refs[i][...] = m_new[i]
    return hs


def _mlstm_body(ql_ref, kl_ref, vl_ref, ol_ref, qc_ref, kc_ref, vc_ref, oc_ref, gl_ref, gc_ref, gb_ref,
                cos_ref, sin_ref, ng_ref, *rest, need_ctx):
    nchain = 2 * REC_HP
    n_out = 2 if need_ctx else 1
    y_ref = rest[0]
    yc_ref = rest[1] if need_ctx else None
    c_scr = rest[n_out:n_out + nchain]
    n_scr = rest[n_out + nchain:n_out + 2 * nchain]
    m_scr = rest[n_out + 2 * nchain:n_out + 3 * nchain]
    hl = rest[n_out + 3 * nchain]
    hc = rest[n_out + 3 * nchain + 1] if need_ctx else None
    for ref in (*c_scr, *n_scr, *m_scr):
        ref[...] = jnp.zeros_like(ref)

    def run(q_ref, k_ref, v_ref, g_ref, h_scr, rope):
        nc = g_ref.shape[3]

        def body(j, carry):
            items, where = [], []
            for hh in range(REC_HP):
                lanes = slice(hh * LANE, (hh + 1) * LANE)
                for d in range(2):
                    jj = nc - 1 - j if d else j
                    off = pl.multiple_of(jj * CHUNK, CHUNK)
                    q = q_ref[0, pl.ds(off, CHUNK), lanes]
                    k = k_ref[0, pl.ds(off, CHUNK), lanes]
                    v = v_ref[0, pl.ds(off, CHUNK), lanes]
                    if rope:
                        cs = cos_ref[pl.ds(off, CHUNK), :]
                        sn = sin_ref[pl.ds(off, CHUNK), :]
                        q = q * cs + pltpu.roll(q, ML_HEAD_DIM // 2, 1) * sn
                        k = k * cs + pltpu.roll(k, ML_HEAD_DIM // 2, 1) * sn
                    q = q * (ML_HEAD_DIM ** -0.5)
                    i_row = g_ref[0, hh, 2 * d, pl.ds(jj, 1), :] + gb_ref[hh, 2 * d]
                    f_row = _log_sigmoid(g_ref[0, hh, 2 * d + 1, pl.ds(jj, 1), :] + gb_ref[hh, 2 * d + 1])
                    items.append((q, k, v, i_row, f_row, bool(d)))
                    where.append((d, off, lanes))
            hs = _ml_steps(items, c_scr, n_scr, m_scr)
            if h_scr is not None:
                for h, (d, off, lanes) in zip(hs, where):
                    h_scr[d, pl.ds(off, CHUNK), lanes] = h
            return carry

        lax.fori_loop(0, nc, body, 0, unroll=REC_UNROLL)

    def finish(h_scr, o_ref, out_ref):
        for hh in range(REC_HP):
            lanes = slice(hh * LANE, (hh + 1) * LANE)
            h = h_scr[0, :, lanes] + h_scr[1, :, lanes]
            dev = h - jnp.mean(h, axis=-1, keepdims=True)
            var = jnp.mean(dev * dev, axis=-1, keepdims=True)
            out_ref[0, :, lanes] = dev * lax.rsqrt(var + EPS) * ng_ref[:, lanes] * _sigmoid(o_ref[0, :, lanes])

    run(qc_ref, kc_ref, vc_ref, gc_ref, hc, False)
    run(ql_ref, kl_ref, vl_ref, gl_ref, hl, True)
    finish(hl, ol_ref, y_ref)
    if need_ctx:
        finish(hc, oc_ref, yc_ref)


def _rope_tables(L):
    n_freq = ML_HEAD_DIM // 4
    t = jnp.arange(L)
    row = (t // GRID_W).astype(F32)
    col = (t % GRID_W).astype(F32)
    inv = ROPE_BASE ** (-jnp.arange(n_freq, dtype=F32) / n_freq)
    ang = jnp.concatenate([row[:, None] * inv, col[:, None] * inv], axis=-1)
    c, s = jnp.cos(ang), jnp.sin(ang)
    return jnp.concatenate([c, c], axis=-1), jnp.concatenate([-s, s], axis=-1)


def _gate_rows(g, n_types, heads):
    B, L, _ = g.shape
    return g.reshape(B, L, n_types, heads).transpose(0, 3, 2, 1).reshape(B, heads, n_types, L // CHUNK, CHUNK)


def _mlstm(p, pc, gate_b, norm_g, need_ctx):
    B, L, _ = p.shape
    Lc = pc.shape[1]
    H = ML_HEADS
    gl = _gate_rows(p[:, :, SMALL * LANE:SMALL * LANE + 4 * H], 4, H)
    gc = _gate_rows(pc[:, :, SMALL * LANE:SMALL * LANE + 4 * H], 4, H)
    gb = gate_b.T.reshape(H, 4, 1, 1)
    cos, sin = _rope_tables(L)
    W = REC_HP * LANE
    nchain = 2 * REC_HP
    lat = lambda col: pl.BlockSpec((1, L, W), lambda b, h: (b, 0, col // REC_HP + h))
    ctx = lambda col: pl.BlockSpec((1, Lc, W), lambda b, h: (b, 0, col // REC_HP + h))
    gspec = lambda n: pl.BlockSpec((1, REC_HP, 4, n // CHUNK, CHUNK), lambda b, h: (b, h, 0, 0, 0))
    fix = lambda b, h: (0, 0)
    out_specs = [pl.BlockSpec((1, L, W), lambda b, h: (b, 0, h))]
    out_shape = [jax.ShapeDtypeStruct((B, L, GROUP_W), F32)]
    scratch = ([pltpu.VMEM((ML_HEAD_DIM, ML_HEAD_DIM), F32)] * nchain + [pltpu.VMEM((1, ML_HEAD_DIM), F32)] * nchain
               + [pltpu.VMEM((1, 1), F32)] * nchain + [pltpu.VMEM((2, L, W), F32)])
    if need_ctx:
        out_specs.append(pl.BlockSpec((1, Lc, W), lambda b, h: (b, 0, h)))
        out_shape.append(jax.ShapeDtypeStruct((B, Lc, GROUP_W), F32))
        scratch.append(pltpu.VMEM((2, Lc, W), F32))
    res = pl.pallas_call(
        functools.partial(_mlstm_body, need_ctx=need_ctx),
        grid=(B, H // REC_HP),
        in_specs=[lat(ML_Q), lat(ML_K), lat(ML_V), lat(ML_O), ctx(ML_Q), ctx(ML_K), ctx(ML_V), ctx(ML_O),
                  gspec(L), gspec(Lc), pl.BlockSpec((REC_HP, 4, 1, 1), lambda b, h: (h, 0, 0, 0)),
                  pl.BlockSpec((L, LANE), fix), pl.BlockSpec((L, LANE), fix),
                  pl.BlockSpec((1, W), lambda b, h: (0, h))],
        out_specs=out_specs,
        out_shape=out_shape,
        scratch_shapes=scratch,
        compiler_params=_params(("arbitrary", "arbitrary"), 40),
        name="mlstm",
    )(p, p, p, p, pc, pc, pc, pc, gl, gc, gb, cos, sin, norm_g[None, :])
    return (res[0], res[1]) if need_ctx else (res[0], None)


def _dn_prep(items):
    n = range(len(items))
    q, k, v, a_row, beta_row, rev = zip(*items)
    masks = [_chunk_masks(r) for r in rev]
    incl, mask, strict, eye = zip(*masks)
    g_row = [jnp.dot(a_row[i], incl[i], preferred_element_type=F32, precision=HI) for i in n]
    g_col = [_to_col(g_row[i], eye[i]) for i in n]
    beta_col = [_to_col(beta_row[i], eye[i]) for i in n]
    dec = [jnp.exp(jnp.where(mask[i], g_col[i] - g_row[i], -jnp.inf)) for i in n]
    kbeta = [k[i] * beta_col[i] for i in n]
    eg_col = [jnp.exp(g_col[i]) for i in n]
    kk = [_bdot_g(kbeta[i], k[i], NT_DIMS) for i in n]
    x = [-jnp.where(strict[i], kk[i] * dec[i], 0.0) for i in n]
    inv = [eye[i].astype(F32) + x[i] for i in n]
    xs = [_split_bf16(x[i]) for i in n]
    for _ in range(5):
        x = [_dot3(xs[i], xs[i]) for i in n]
        xs = [_split_bf16(x[i]) for i in n]
        invs = [_split_bf16(inv[i]) for i in n]
        inv = [inv[i] + _dot3(invs[i], xs[i]) for i in n]
    invs = [_split_bf16(inv[i]) for i in n]
    u = [_dot3(invs[i], _split_bf16(v[i] * beta_col[i])) for i in n]
    w = [_dot3(invs[i], _split_bf16(kbeta[i] * eg_col[i])) for i in n]
    attn = [_bdot_g(q[i], k[i], NT_DIMS) * dec[i] for i in n]
    g_last = [g_row[i][:, 0:1] if rev[i] else g_row[i][:, CHUNK - 1:CHUNK] for i in n]
    qe = [q[i] * eg_col[i] for i in n]
    kd = [k[i] * jnp.exp(g_last[i] - g_col[i]) for i in n]
    return [(u[i], w[i], attn[i], qe[i], kd[i], jnp.exp(g_last[i])) for i in n]


def _dn_apply(preps, s_refs):
    n = range(len(preps))
    u, w, attn, qe, kd, eg_last = zip(*preps)
    st = [s_refs[i][...] for i in n]
    ws = [_bdot(w[i], st[i]) for i in n]
    qs = [_bdot(qe[i], st[i]) for i in n]
    v_new = [u[i] - ws[i] for i in n]
    av = [_bdot(attn[i], v_new[i]) for i in n]
    kv = [_bdot_g(kd[i], v_new[i], TN_DIMS) for i in n]
    for i in n:
        s_refs[i][...] = eg_last[i] * st[i] + kv[i]
    return [qs[i] + av[i] for i in n]


def _deltanet_body(ql_ref, kl_ref, vl_ref, zl_ref, qc_ref, kc_ref, vc_ref, zc_ref, gl_ref, gc_ref, prm_ref,
                   ng_ref, *rest, need_ctx):
    nchain = 2 * REC_HP
    n_out = 2 if need_ctx else 1
    y_ref = rest[0]
    yc_ref = rest[1] if need_ctx else None
    s_scr = rest[n_out:n_out + nchain]
    ol = rest[n_out + nchain]
    oc = rest[n_out + nchain + 1] if need_ctx else None
    for ref in s_scr:
        ref[...] = jnp.zeros_like(ref)

    def unit(x):
        return x * lax.rsqrt(jnp.sum(x * x, axis=-1, keepdims=True) + EPS)

    def run(q_ref, k_ref, v_ref, g_ref, o_scr):
        nc = g_ref.shape[3]

        chains = [(hh, d) for hh in range(REC_HP) for d in range(2)]

        def body(j2, carry):
            items, where = [], []
            for t in range(REC_UNROLL):
                j = j2 * REC_UNROLL + t
                for hh, d in chains:
                    lanes = slice(hh * LANE, (hh + 1) * LANE)
                    jj = nc - 1 - j if d else j
                    off = pl.multiple_of(jj * CHUNK, CHUNK)
                    q = unit(q_ref[0, pl.ds(off, CHUNK), lanes]) * (DN_HEAD_DIM ** -0.5)
                    k = unit(k_ref[0, pl.ds(off, CHUNK), lanes])
                    v = v_ref[0, pl.ds(off, CHUNK), lanes]
                    a_raw = g_ref[0, hh, 2 * d, pl.ds(jj, 1), :]
                    a_row = -jnp.exp(prm_ref[hh, 2 * d]) * _softplus(a_raw + prm_ref[hh, 2 * d + 1])
                    beta_row = _sigmoid(g_ref[0, hh, 2 * d + 1, pl.ds(jj, 1), :])
                    items.append((q, k, v, a_row, beta_row, bool(d)))
                    where.append((d, off, lanes))
            preps = _dn_prep(items)
            nch = len(chains)
            for t in range(REC_UNROLL):
                outs = _dn_apply(preps[t * nch:(t + 1) * nch], s_scr)
                if o_scr is not None:
                    for o, (d, off, lanes) in zip(outs, where[t * nch:(t + 1) * nch]):
                        o_scr[d, pl.ds(off, CHUNK), lanes] = o
            return carry

        lax.fori_loop(0, nc // REC_UNROLL, body, 0)

    def finish(o_scr, z_ref, out_ref):
        for hh in range(REC_HP):
            lanes = slice(hh * LANE, (hh + 1) * LANE)
            o = o_scr[0, :, lanes] + o_scr[1, :, lanes]
            z = z_ref[0, :, lanes]
            out_ref[0, :, lanes] = o * lax.rsqrt(_mean_sq(o) + EPS) * ng_ref[...] * (z * _sigmoid(z))

    run(qc_ref, kc_ref, vc_ref, gc_ref, oc)
    run(ql_ref, kl_ref, vl_ref, gl_ref, ol)
    finish(ol, zl_ref, y_ref)
    if need_ctx:
        finish(oc, zc_ref, yc_ref)


def _deltanet(p, pc, conv_w, a_log, dt_bias, norm_g, need_ctx):
    B, L, _ = p.shape
    Lc = pc.shape[1]
    H = DN_HEADS
    zero_b = jnp.zeros((1, 3 * GROUP_W), F32)
    u = _conv3(p, DN_Q, conv_w, zero_b, silu=True)
    uc = _conv3(pc, DN_Q, conv_w, zero_b, silu=True)

    def gates(src):
        base = SMALL * LANE + 4 * ML_HEADS
        beta = src[:, :, base:base + 2 * H].reshape(src.shape[0], src.shape[1], 2, 1, H)
        a = src[:, :, base + 2 * H:base + 4 * H].reshape(src.shape[0], src.shape[1], 2, 1, H)
        both = jnp.concatenate([a, beta], axis=3).reshape(src.shape[0], src.shape[1], 4 * H)
        return _gate_rows(both, 4, H)

    prm = jnp.stack([a_log[0], dt_bias[0], a_log[1], dt_bias[1]], axis=1).reshape(H, 4, 1, 1)
    W = REC_HP * LANE
    lat = lambda blk: pl.BlockSpec((1, L, W), lambda b, h: (b, 0, blk // REC_HP + h))
    ctx = lambda blk: pl.BlockSpec((1, Lc, W), lambda b, h: (b, 0, blk // REC_HP + h))
    gspec = lambda n: pl.BlockSpec((1, REC_HP, 4, n // CHUNK, CHUNK), lambda b, h: (b, h, 0, 0, 0))
    out_specs = [pl.BlockSpec((1, L, W), lambda b, h: (b, 0, h))]
    out_shape = [jax.ShapeDtypeStruct((B, L, GROUP_W), F32)]
    scratch = [pltpu.VMEM((DN_HEAD_DIM, DN_HEAD_DIM), F32)] * (2 * REC_HP) + [pltpu.VMEM((2, L, W), F32)]
    if need_ctx:
        out_specs.append(pl.BlockSpec((1, Lc, W), lambda b, h: (b, 0, h)))
        out_shape.append(jax.ShapeDtypeStruct((B, Lc, GROUP_W), F32))
        scratch.append(pltpu.VMEM((2, Lc, W), F32))
    res = pl.pallas_call(
        functools.partial(_deltanet_body, need_ctx=need_ctx),
        grid=(B, H // REC_HP),
        in_specs=[lat(0), lat(4), lat(8), lat(DN_G), ctx(0), ctx(4), ctx(8), ctx(DN_G),
                  gspec(L), gspec(Lc), pl.BlockSpec((REC_HP, 4, 1, 1), lambda b, h: (h, 0, 0, 0)),
                  pl.BlockSpec((1, LANE), lambda b, h: (0, 0))],
        out_specs=out_specs,
        out_shape=out_shape,
        scratch_shapes=scratch,
        compiler_params=_params(("arbitrary", "arbitrary"), 40),
        name="deltanet",
    )(u, u, u, p, uc, uc, uc, pc, gates(p), gates(pc), prm, norm_g[None, :])
    return (res[0], res[1]) if need_ctx else (res[0], None)


def _permute_w_in(w):
    D = w.shape[0]
    G = GROUP_W
    n_gate = 4 * ML_HEADS
    o = 0
    na, o = w[:, o:o + 3 * G], o + 3 * G
    ml_qkv, o = w[:, o:o + 3 * G], o + 3 * G
    ml_o, o = w[:, o:o + G], o + G
    ml_gates, o = w[:, o:o + n_gate], o + n_gate
    hy, o = w[:, o:o + 3 * G], o + 3 * G
    dn_qkv, o = w[:, o:o + 3 * G], o + 3 * G
    dn_gate, o = w[:, o:o + G], o + G
    dn_small = w[:, o:o + 4 * DN_HEADS]

    def split_pairs(m):
        return m.reshape(D, ML_HEADS, ML_HEAD_DIM // 2, 2).transpose(0, 1, 3, 2).reshape(D, G)

    used = 56 * LANE + n_gate + 4 * DN_HEADS
    parts = [na, split_pairs(ml_qkv[:, :G]), split_pairs(ml_qkv[:, G:2 * G]), ml_qkv[:, 2 * G:], ml_o, hy, dn_qkv,
             dn_gate, ml_gates, dn_small, jnp.zeros((D, P_COLS - used), w.dtype)]
    return jnp.concatenate(parts, axis=1).astype(BF16)


def kernel(x, c, ctx, c_ctx, w_ada, b_ada, norm_g, w_in, w_out, na_rpb, ml_gate_b, ml_norm_g, hy_conv_w, hy_conv_b, hy_f_w1, hy_f_b1, hy_f_w2, hy_f_b2, hy_f_w3, hy_f_b3, hy_skip, dn_conv_w, dn_a_log, dn_dt_bias, dn_norm_g, router_w, router_b, moe_w_gu, moe_b_gu, moe_w_down, moe_b_down):
    B, L, D = x.shape
    Lc = ctx.shape[1]
    assert D == D_MODEL and B < 16
    cc = jnp.zeros((16, D), F32).at[:B].set(c).at[B].set(c_ctx)
    mods = _adaln(cc, w_ada, b_ada)
    x2 = x.reshape(B * L, D)
    xc2 = ctx.reshape(B * Lc, D)
    flat = lambda y: y.reshape(-1, GROUP_W)
    for l in range(DEPTH):
        need_ctx = l < DEPTH - 1
        mod = mods[l, :B].reshape(B, 6, 1, D)
        mod_c = mods[l, B].reshape(1, 6, 1, D)
        g = norm_g[l]
        w_p = _permute_w_in(w_in[l])
        p = _inproj(x2, L, mod, g[0:1], w_p).reshape(B, L, P_COLS)
        pc = _inproj(xc2, Lc, mod_c, g[0:1], w_p).reshape(B, Lc, P_COLS)
        hy = (hy_conv_w[l], hy_conv_b[l], hy_f_w1[l], hy_f_b1[l], hy_f_w2[l], hy_f_b2[l], hy_f_w3[l], hy_f_b3[l],
              hy_skip[l])
        ya = _na_attention(p, pc, _na_bias_table(na_rpb[l]))
        yb, ybc = _mlstm(p, pc, ml_gate_b[l], ml_norm_g[l], need_ctx)
        yh = _hyena(p, *hy)
        yd, ydc = _deltanet(p, pc, dn_conv_w[l], dn_a_log[l], dn_dt_bias[l], dn_norm_g[l], need_ctx)
        w_out_bf = w_out[l].astype(BF16)
        rb = router_b[l][None, :]
        x2, tok, logits = _outproj([flat(ya), flat(yb), flat(yh), flat(yd)], w_out_bf, x2, L, mod, g[1:2], g[2:3],
                                   router_w[l], rb)
        if need_ctx:
            yac = _ctx_attention(pc)
            yhc = _hyena(pc, *hy)
            xc2, tok_c, logits_c = _outproj([flat(yac), flat(ybc), flat(yhc), flat(ydc)], w_out_bf, xc2, Lc, mod_c,
                                            g[1:2], g[2:3], router_w[l], rb)
            tok = jnp.concatenate([tok, tok_c], axis=0)
            logits = jnp.concatenate([logits, logits_c], axis=0)
        ps, idx3, rank3, gates, ys = _moe(tok, logits, moe_w_gu, moe_b_gu, moe_w_down, moe_b_down, l)
        nl = B * L // MOE_TM
        x2 = _combine(ps, idx3[:nl], rank3[:nl], gates[:B * L], ys, x2, L, mod, g[3:4])
        if need_ctx:
            xc2 = _combine(ps, idx3[nl:], rank3[nl:], gates[B * L:], ys, xc2, Lc, mod_c, g[3:4])
    return x2.reshape(B, L, D)
```

```python
import functools
import math

import numpy as np
import jax
import jax.numpy as jnp
from jax import lax
from jax.experimental import pallas as pl
from jax.experimental.pallas import tpu as pltpu

F32 = jnp.float32
BF16 = jnp.bfloat16
I32 = jnp.int32

D_MODEL = 2048
DEPTH = 2
GRID_W = 64
N_MIXERS = 4
GROUP_W = D_MODEL // N_MIXERS
NA_HEADS = 8
NA_HEAD_DIM = GROUP_W // NA_HEADS
NA_WIN_H = 8
NA_WIN_W = 16
ML_HEADS = 4
ML_HEAD_DIM = GROUP_W // ML_HEADS
HY_CH = GROUP_W
HY_BANDS = 8
HY_HIDDEN = 64
HY_SIN_FREQ = 1.0
HY_FAST_DECAY = 0.3
HY_SLOW_DECAY = 1.5
HY_DECAY_TARGET = 1e-2
DN_HEADS = 4
DN_HEAD_DIM = GROUP_W // DN_HEADS
CHUNK = 64
N_EXPERTS = 32
TOP_K = 4
D_FF = D_MODEL
SWIGLU_LIMIT = 7.0
SWIGLU_ALPHA = 1.702
MOE_BLOCK = 256
ROPE_BASE = 10000.0
EPS = 1e-6

LANE = 128
HALF = D_MODEL // 2
NEG = -1e30
HI = lax.Precision.HIGHEST

NA_Q, NA_K, NA_V = 0, 4, 8
ML_Q, ML_K, ML_V, ML_O = 12, 16, 20, 24
HY_U = 28
DN_Q = 40
DN_G = 52
SMALL = 56
P_COLS = 60 * LANE


def _params(sem, vmem_mb):
    return pltpu.CompilerParams(dimension_semantics=sem, vmem_limit_bytes=vmem_mb << 20)


def _mean_sq(x):
    return jnp.mean(x * x, axis=-1, keepdims=True)


def _sigmoid(x):
    return 1.0 / (1.0 + jnp.exp(-x))


def _adaln_body(c_ref, w_ref, b_ref, o_ref):
    c = c_ref[...]
    s = (c * _sigmoid(c)).astype(BF16)
    o_ref[0] = jnp.dot(s, w_ref[0].astype(BF16), preferred_element_type=F32) + b_ref[0]


def _adaln(cc, w_ada, b_ada):
    D = cc.shape[1]
    tn = 1024
    return pl.pallas_call(
        _adaln_body,
        grid=(DEPTH, 6 * D // tn),
        in_specs=[pl.BlockSpec((16, D), lambda l, j: (0, 0)),
                  pl.BlockSpec((1, D, tn), lambda l, j: (l, 0, j)),
                  pl.BlockSpec((1, 1, tn), lambda l, j: (l, 0, j))],
        out_specs=pl.BlockSpec((1, 16, tn), lambda l, j: (l, 0, j)),
        out_shape=jax.ShapeDtypeStruct((DEPTH, 16, 6 * D), F32),
        compiler_params=_params(("arbitrary", "arbitrary"), 40),
        name="adaln",
    )(cc, w_ada, b_ada.reshape(DEPTH, 1, 6 * D))


def _inproj_body(x_ref, g_ref, sc_ref, sh_ref, w_ref, o_ref, h_scr):
    @pl.when(pl.program_id(1) == 0)
    def _():
        x = x_ref[...]
        y = x * lax.rsqrt(_mean_sq(x) + EPS) * g_ref[...]
        h_scr[...] = (y * (1.0 + sc_ref[0, 0]) + sh_ref[0, 0]).astype(BF16)

    o_ref[...] = jnp.dot(h_scr[...], w_ref[...], preferred_element_type=F32)


def _inproj(x2, seq_len, mod, g, w_p):
    T, D = x2.shape
    tm = min(1024, seq_len)
    tn = 512
    per_seq = seq_len // tm
    bmap = (lambda i: i // per_seq) if mod.shape[0] > 1 else (lambda i: 0)
    return pl.pallas_call(
        _inproj_body,
        grid=(T // tm, P_COLS // tn),
        in_specs=[pl.BlockSpec((tm, D), lambda i, j: (i, 0)),
                  pl.BlockSpec((1, D), lambda i, j: (0, 0)),
                  pl.BlockSpec((1, 1, 1, D), lambda i, j: (bmap(i), 1, 0, 0)),
                  pl.BlockSpec((1, 1, 1, D), lambda i, j: (bmap(i), 0, 0, 0)),
                  pl.BlockSpec((D, tn), lambda i, j: (0, j))],
        out_specs=pl.BlockSpec((tm, tn), lambda i, j: (i, j)),
        out_shape=jax.ShapeDtypeStruct((T, P_COLS), F32),
        scratch_shapes=[pltpu.VMEM((tm, D), BF16)],
        compiler_params=_params(("arbitrary", "arbitrary"), 48),
        name="inproj",
    )(x2, g, mod, mod, w_p)


def _pack_halves(t):
    hi = pltpu.bitcast(t[:, :HALF].astype(BF16).astype(F32), I32)
    lo = pltpu.bitcast(t[:, HALF:].astype(BF16).astype(F32), I32)
    return hi | lax.shift_right_logical(lo, 16)


def _unpack_halves(u):
    hi = pltpu.bitcast(u & jnp.int32(-65536), F32)
    lo = pltpu.bitcast(lax.shift_left(u, 16), F32)
    return hi, lo


def _outproj_body(ya_ref, yb_ref, yh_ref, yd_ref, w_ref, x_ref, g1n_ref, gate_ref, g2n_ref, sc_ref, sh_ref,
                  rw_ref, rb_ref, xo_ref, tok_ref, lg_ref):
    G = GROUP_W
    tm = x_ref.shape[0]
    for half in range(2):
        rows = slice(half * tm // 2, (half + 1) * tm // 2)
        y = jnp.dot(ya_ref[rows, :].astype(BF16), w_ref[0:G, :], preferred_element_type=F32)
        y += jnp.dot(yb_ref[rows, :].astype(BF16), w_ref[G:2 * G, :], preferred_element_type=F32)
        y += jnp.dot(yh_ref[rows, :].astype(BF16), w_ref[2 * G:3 * G, :], preferred_element_type=F32)
        y += jnp.dot(yd_ref[rows, :].astype(BF16), w_ref[3 * G:4 * G, :], preferred_element_type=F32)
        yn = y * lax.rsqrt(_mean_sq(y) + EPS) * g1n_ref[...]
        xn = x_ref[rows, :] + gate_ref[0, 0] * yn
        xo_ref[rows, :] = xn
        t = xn * lax.rsqrt(_mean_sq(xn) + EPS) * g2n_ref[...]
        t = t * (1.0 + sc_ref[0, 0]) + sh_ref[0, 0]
        tok_ref[rows, :] = _pack_halves(t)
        lg_ref[rows, :] = jnp.dot(t, rw_ref[...], preferred_element_type=F32, precision=HI) + rb_ref[...]


def _outproj(ys, w_out_bf, x2, seq_len, mod, g1n, g2n, router_w, router_b):
    T, D = x2.shape
    tm = 256
    per_seq = seq_len // tm
    bmap = (lambda i: i // per_seq) if mod.shape[0] > 1 else (lambda i: 0)
    row = lambda i: (i, 0)
    fix = lambda i: (0, 0)
    modspec = lambda k: pl.BlockSpec((1, 1, 1, D), lambda i: (bmap(i), k, 0, 0))
    return pl.pallas_call(
        _outproj_body,
        grid=(T // tm,),
        in_specs=[pl.BlockSpec((tm, GROUP_W), row)] * 4 + [
            pl.BlockSpec((D, D), fix), pl.BlockSpec((tm, D), row), pl.BlockSpec((1, D), fix), modspec(2),
            pl.BlockSpec((1, D), fix), modspec(4), modspec(3),
            pl.BlockSpec((D, N_EXPERTS), fix), pl.BlockSpec((1, N_EXPERTS), fix)],
        out_specs=[pl.BlockSpec((tm, D), row), pl.BlockSpec((tm, HALF), row), pl.BlockSpec((tm, N_EXPERTS), row)],
        out_shape=[jax.ShapeDtypeStruct((T, D), F32), jax.ShapeDtypeStruct((T, HALF), I32),
                   jax.ShapeDtypeStruct((T, N_EXPERTS), F32)],
        compiler_params=_params(("arbitrary",), 48),
        name="outproj",
    )(*ys, w_out_bf, x2, g1n, mod, g2n, mod, mod, router_w, router_b)


def _lane_select(cols, width, shape):
    lane = lax.broadcasted_iota(I32, shape, 1)
    out = jnp.broadcast_to(cols[width - 1], shape)
    for k in range(width - 2, -1, -1):
        out = jnp.where(lane == k, cols[k], out)
    return out


def _router_body(l_ref, idx_ref, rank_ref, gate_ref, cnt_ref, carry):
    @pl.when(pl.program_id(0) == 0)
    def _():
        carry[...] = jnp.zeros_like(carry)

    cur = l_ref[...]
    tm = cur.shape[0]
    lane = lax.broadcasted_iota(I32, cur.shape, 1)
    vals, ids, hots = [], [], []
    for _ in range(TOP_K):
        m = jnp.max(cur, axis=-1, keepdims=True)
        sel = jnp.min(jnp.where(cur == m, lane, N_EXPERTS), axis=-1, keepdims=True)
        hot = lane == sel
        vals.append(m)
        ids.append(sel)
        hots.append(hot)
        cur = jnp.where(hot, -jnp.inf, cur)
    es = [jnp.exp(v - vals[0]) for v in vals]
    tot = es[0] + es[1] + es[2] + es[3]
    cnt = sum(h.astype(F32) for h in hots)
    r = lax.broadcasted_iota(I32, (tm, tm), 0)
    c = lax.broadcasted_iota(I32, (tm, tm), 1)
    before = (c < r).astype(BF16)
    pre = jnp.dot(before, cnt.astype(BF16), preferred_element_type=F32) + carry[...]
    ranks = [jnp.sum(jnp.where(h, pre, 0.0), axis=-1, keepdims=True).astype(I32) for h in hots]
    carry[...] += jnp.sum(cnt, axis=0, keepdims=True)
    shape = (tm, TOP_K)
    idx_ref[...] = _lane_select(ids, TOP_K, shape)
    rank_ref[...] = _lane_select(ranks, TOP_K, shape)
    gate_ref[...] = _lane_select([e / tot for e in es], TOP_K, shape)
    cnt_ref[...] = carry[...].astype(I32)


def _router(logits):
    T = logits.shape[0]
    tm = 256
    row = lambda i: (i, 0)
    return pl.pallas_call(
        _router_body,
        grid=(T // tm,),
        in_specs=[pl.BlockSpec((tm, N_EXPERTS), row)],
        out_specs=[pl.BlockSpec((tm, TOP_K), row), pl.BlockSpec((tm, TOP_K), row), pl.BlockSpec((tm, TOP_K), row),
                   pl.BlockSpec((1, N_EXPERTS), lambda i: (0, 0))],
        out_shape=[jax.ShapeDtypeStruct((T, TOP_K), I32), jax.ShapeDtypeStruct((T, TOP_K), I32),
                   jax.ShapeDtypeStruct((T, TOP_K), F32), jax.ShapeDtypeStruct((1, N_EXPERTS), I32)],
        scratch_shapes=[pltpu.VMEM((1, N_EXPERTS), F32)],
        compiler_params=_params(("arbitrary",), 32),
        name="router",
    )(logits)


MOE_TM = 256


def _slot_copy(src_ref, dst_ref, src_row, dst_row, sem):
    return pltpu.make_async_copy(src_ref.at[pl.ds(src_row, 1)], dst_ref.at[pl.ds(dst_row, 1)], sem)


def _dispatch_body(dest_ref, tok_ref, xs_in_ref, xs_ref, sem):
    del xs_in_ref

    def start(r, carry):
        for k in range(TOP_K):
            _slot_copy(tok_ref, xs_ref, r, dest_ref[0, 0, r * TOP_K + k], sem).start()
        return carry

    lax.fori_loop(0, MOE_TM, start, 0, unroll=2)

    def wait(r, carry):
        for k in range(TOP_K):
            _slot_copy(tok_ref, xs_ref, 0, 0, sem).wait()
        return carry

    lax.fori_loop(0, MOE_TM, wait, 0, unroll=2)


def _dest_spec():
    return pl.BlockSpec((1, 1, MOE_TM * TOP_K), lambda i: (i, 0, 0), memory_space=pltpu.SMEM)


def _dispatch(dest3, tok, n_slots):
    T = tok.shape[0]
    tm = MOE_TM
    return pl.pallas_call(
        _dispatch_body,
        grid=(T // tm,),
        in_specs=[_dest_spec(), pl.BlockSpec((tm, HALF), lambda i: (i, 0)), pl.BlockSpec(memory_space=pl.ANY)],
        out_specs=pl.BlockSpec(memory_space=pl.ANY),
        scratch_shapes=[pltpu.SemaphoreType.DMA(())],
        out_shape=jax.ShapeDtypeStruct((n_slots, HALF), I32),
        input_output_aliases={2: 0},
        compiler_params=_params(("arbitrary",), 32),
        name="moe_dispatch",
    )(dest3, tok, jnp.zeros((n_slots, HALF), I32))


def _fresh_weights(be_ref, i):
    return jnp.logical_or(i == 0, be_ref[i] != be_ref[jnp.maximum(i - 1, 0)])


def _g1_body(be_ref, nu_ref, x_ref, wg_ref, wu_ref, bg_ref, bu_ref, h_ref, wg_bf, wu_bf):
    i = pl.program_id(1)
    used = i < nu_ref[0]

    @pl.when(jnp.logical_and(used, _fresh_weights(be_ref, i)))
    def _():
        wg_bf[...] = wg_ref[0, 0].astype(BF16)
        wu_bf[...] = wu_ref[0, 0].astype(BF16)

    @pl.when(used)
    def _():
        hi, lo = _unpack_halves(x_ref[...])
        hi = hi.astype(BF16)
        lo = lo.astype(BF16)

        def proj(w_bf, b_ref):
            return (jnp.dot(hi, w_bf[:HALF, :], preferred_element_type=F32)
                    + jnp.dot(lo, w_bf[HALF:, :], preferred_element_type=F32) + b_ref[0, 0])

        g = jnp.minimum(proj(wg_bf, bg_ref), SWIGLU_LIMIT)
        u = jnp.clip(proj(wu_bf, bu_ref), -SWIGLU_LIMIT, SWIGLU_LIMIT)
        h_ref[...] = (g * _sigmoid(SWIGLU_ALPHA * g) * (u + 1.0)).astype(BF16)

    @pl.when(jnp.logical_not(used))
    def _():
        h_ref[...] = jnp.zeros_like(h_ref)


def _expert_up(block_expert, n_used, xs, w_gu, b_gu4, layer):
    n_slots = xs.shape[0]
    nb = n_slots // MOE_BLOCK
    tn = 1024
    nj = D_FF // tn
    last = lambda i, nu: jnp.minimum(i, nu[0] - 1)
    wspec = lambda off: pl.BlockSpec((1, 1, D_MODEL, tn), lambda j, i, be, nu: (layer, be[last(i, nu)], 0, off + j))
    bspec = lambda off: pl.BlockSpec((1, 1, 1, tn), lambda j, i, be, nu: (layer, be[last(i, nu)], 0, off + j))
    return pl.pallas_call(
        _g1_body,
        grid_spec=pltpu.PrefetchScalarGridSpec(
            num_scalar_prefetch=2,
            grid=(nj, nb),
            in_specs=[pl.BlockSpec((MOE_BLOCK, HALF), lambda j, i, be, nu: (last(i, nu), 0)),
                      wspec(0), wspec(nj), bspec(0), bspec(nj)],
            out_specs=pl.BlockSpec((MOE_BLOCK, tn), lambda j, i, be, nu: (i, j)),
            scratch_shapes=[pltpu.VMEM((D_MODEL, tn), BF16), pltpu.VMEM((D_MODEL, tn), BF16)]),
        out_shape=jax.ShapeDtypeStruct((n_slots, D_FF), BF16),
        compiler_params=_params(("arbitrary", "arbitrary"), 56),
        name="moe_up",
    )(block_expert, n_used, xs, w_gu, w_gu, b_gu4, b_gu4)


def _g2_body(be_ref, nu_ref, h_ref, w_ref, b_ref, y_ref, w_bf):
    i = pl.program_id(0)
    used = i < nu_ref[0]

    @pl.when(jnp.logical_and(used, _fresh_weights(be_ref, i)))
    def _():
        w_bf[...] = w_ref[0, 0].astype(BF16)

    @pl.when(used)
    def _():
        y_ref[...] = _pack_halves(jnp.dot(h_ref[...], w_bf[...], preferred_element_type=F32) + b_ref[0, 0])

    @pl.when(jnp.logical_not(used))
    def _():
        y_ref[...] = jnp.zeros_like(y_ref)


def _expert_down(block_expert, n_used, h, w_down, b_down4, layer):
    n_slots = h.shape[0]
    nb = n_slots // MOE_BLOCK
    last = lambda i, nu: jnp.minimum(i, nu[0] - 1)
    return pl.pallas_call(
        _g2_body,
        grid_spec=pltpu.PrefetchScalarGridSpec(
            num_scalar_prefetch=2,
            grid=(nb,),
            in_specs=[pl.BlockSpec((MOE_BLOCK, D_FF), lambda i, be, nu: (last(i, nu), 0)),
                      pl.BlockSpec((1, 1, D_FF, D_MODEL), lambda i, be, nu: (layer, be[last(i, nu)], 0, 0)),
                      pl.BlockSpec((1, 1, 1, D_MODEL), lambda i, be, nu: (layer, be[last(i, nu)], 0, 0))],
            out_specs=pl.BlockSpec((MOE_BLOCK, HALF), lambda i, be, nu: (i, 0)),
            scratch_shapes=[pltpu.VMEM((D_FF, D_MODEL), BF16)]),
        out_shape=jax.ShapeDtypeStruct((n_slots, HALF), I32),
        compiler_params=_params(("arbitrary",), 56),
        name="moe_down",
    )(block_expert, n_used, h, w_down, b_down4)


def _combine_body(dest_ref, gate_ref, ys_ref, x_ref, gn_ref, g2_ref, o_ref, buf, sem):
    def start(r, carry):
        for k in range(TOP_K):
            _slot_copy(ys_ref, buf.at[k], dest_ref[0, 0, r * TOP_K + k], r, sem).start()
        return carry

    lax.fori_loop(0, MOE_TM, start, 0, unroll=2)

    def wait(r, carry):
        for k in range(TOP_K):
            _slot_copy(ys_ref, buf.at[k], 0, 0, sem).wait()
        return carry

    lax.fori_loop(0, MOE_TM, wait, 0, unroll=2)

    gate = gate_ref[...]
    f_hi = f_lo = None
    for k in range(TOP_K):
        hi, lo = _unpack_halves(buf[k])
        gk = gate[:, k:k + 1]
        f_hi = gk * hi if k == 0 else f_hi + gk * hi
        f_lo = gk * lo if k == 0 else f_lo + gk * lo
    ms = (jnp.sum(f_hi * f_hi, axis=-1, keepdims=True) + jnp.sum(f_lo * f_lo, axis=-1, keepdims=True)) / D_MODEL
    scale = lax.rsqrt(ms + EPS)
    gn = gn_ref[...]
    g2 = g2_ref[0, 0]
    o_ref[:, :HALF] = x_ref[:, :HALF] + g2[:, :HALF] * (f_hi * scale * gn[:, :HALF])
    o_ref[:, HALF:] = x_ref[:, HALF:] + g2[:, HALF:] * (f_lo * scale * gn[:, HALF:])


def _combine(dest3, gates, ys, x2, seq_len, mod, gn):
    T, D = x2.shape
    tm = MOE_TM
    per_seq = seq_len // tm
    bmap = (lambda i: i // per_seq) if mod.shape[0] > 1 else (lambda i: 0)
    return pl.pallas_call(
        _combine_body,
        grid=(T // tm,),
        in_specs=[_dest_spec(), pl.BlockSpec((tm, TOP_K), lambda i: (i, 0)),
                  pl.BlockSpec(memory_space=pl.ANY),
                  pl.BlockSpec((tm, D), lambda i: (i, 0)),
                  pl.BlockSpec((1, D), lambda i: (0, 0)),
                  pl.BlockSpec((1, 1, 1, D), lambda i: (bmap(i), 5, 0, 0))],
        out_specs=pl.BlockSpec((tm, D), lambda i: (i, 0)),
        scratch_shapes=[pltpu.VMEM((TOP_K, tm, HALF), I32), pltpu.SemaphoreType.DMA(())],
        out_shape=jax.ShapeDtypeStruct((T, D), F32),
        compiler_params=_params(("arbitrary",), 40),
        name="moe_combine",
    )(dest3, gates, ys, x2, gn, mod)


def _moe(tok, logits, w_gu, b_gu, w_down, b_down, layer):
    T = tok.shape[0]
    idx, rank, gates, counts = _router(logits)
    counts = counts[0]
    padded = (counts + MOE_BLOCK - 1) // MOE_BLOCK * MOE_BLOCK
    pad_end = jnp.cumsum(padded)
    pad_start = (pad_end - padded).astype(I32)
    n_slots = (T * TOP_K // MOE_BLOCK + N_EXPERTS) * MOE_BLOCK
    nb = n_slots // MOE_BLOCK
    blk = jnp.arange(nb, dtype=I32) * MOE_BLOCK
    block_expert = jnp.minimum(jnp.sum((blk[:, None] >= pad_end[None, :]).astype(I32), axis=1), N_EXPERTS - 1)
    n_used = (pad_end[-1:] // MOE_BLOCK).astype(I32)
    hot = idx[:, :, None] == jnp.arange(N_EXPERTS, dtype=I32)[None, None, :]
    dest = jnp.sum(jnp.where(hot, pad_start[None, None, :], 0), axis=-1) + rank
    dest3 = dest.reshape(T // MOE_TM, 1, MOE_TM * TOP_K)
    xs = _dispatch(dest3, tok, n_slots)
    h = _expert_up(block_expert, n_used, xs, w_gu, b_gu.reshape(DEPTH, N_EXPERTS, 1, 2 * D_FF), layer)
    ys = _expert_down(block_expert, n_used, h, w_down, b_down.reshape(DEPTH, N_EXPERTS, 1, D_MODEL), layer)
    return dest3, gates, ys


NA_QT = 4
NA_BAND = 12
NA_ROWS = 32


def _na_bias_table(rpb):
    W = GRID_W
    cols = np.arange(W)
    cs = np.clip(cols - NA_WIN_W // 2, 0, W - NA_WIN_W)
    valid_c = (cols[None, :] >= cs[:, None]) & (cols[None, :] < cs[:, None] + NA_WIN_W)
    dc = np.clip(cols[None, :] - cols[:, None] + NA_WIN_W - 1, 0, 2 * NA_WIN_W - 2)
    tm = jnp.where(valid_c, rpb[:, :, dc], NEG)
    nt = NA_ROWS // NA_QT
    qt = np.arange(nt)
    r = NA_QT * qt[:, None] + np.arange(NA_QT)[None, :]
    kb = np.clip(NA_QT * qt - NA_WIN_H // 2, 0, NA_ROWS - NA_BAND)
    kr = kb[:, None] + np.arange(NA_BAND)[None, :]
    rs = np.clip(r - NA_WIN_H // 2, 0, NA_ROWS - NA_WIN_H)
    valid_r = (kr[:, None, :] >= rs[:, :, None]) & (kr[:, None, :] < rs[:, :, None] + NA_WIN_H)
    dr = np.clip(kr[:, None, :] - r[:, :, None] + NA_WIN_H - 1, 0, 2 * NA_WIN_H - 2)
    blocks = jnp.where(valid_r[None, :, :, :, None, None], tm[:, dr], NEG)
    return blocks.transpose(1, 0, 2, 4, 3, 5).reshape(nt, NA_HEADS, NA_QT * W, NA_BAND * W)


def _masked_heads_attention(q, keys, values, biases):
    lane = lax.broadcasted_iota(I32, q.shape, 1)
    nt = (((1,), (1,)), ((), ()))
    out = None
    for hh in range(2):
        hm = (lane >= NA_HEAD_DIM) == bool(hh)
        qh = jnp.where(hm, q, 0.0).astype(BF16)
        ss = []
        for kk, bb in zip(keys, biases):
            s = lax.dot_general(qh, kk, nt, preferred_element_type=F32)
            ss.append(s if bb is None else s + bb[hh])
        m = ss[0].max(axis=1, keepdims=True)
        for s in ss[1:]:
            m = jnp.maximum(m, s.max(axis=1, keepdims=True))
        den = None
        o = None
        for s, vv in zip(ss, values):
            p = jnp.exp(s - m)
            d = jnp.sum(p, axis=1, keepdims=True)
            den = d if den is None else den + d
            pv = jnp.dot(p.astype(BF16), vv, preferred_element_type=F32)
            o = pv if o is None else o + pv
        o = o / den
        out = o if hh == 0 else jnp.where(hm, o, out)
    return out


def _na_body(q_ref, k_ref, v_ref, kc_ref, vc_ref, b_ref, o_ref):
    qt = pl.program_id(2)
    kb = pl.multiple_of(jnp.clip(qt * NA_QT - NA_WIN_H // 2, 0, NA_ROWS - NA_BAND) * GRID_W, GRID_W)
    band = NA_BAND * GRID_W
    q = q_ref[0] * (NA_HEAD_DIM ** -0.5)
    kl = k_ref[0, pl.ds(kb, band), :].astype(BF16)
    vl = v_ref[0, pl.ds(kb, band), :].astype(BF16)
    o_ref[0] = _masked_heads_attention(q, [kl, kc_ref[0].astype(BF16)], [vl, vc_ref[0].astype(BF16)],
                                       [(b_ref[0, 0], b_ref[0, 1]), None])


def _na_attention(p, pc, bias):
    B, L, _ = p.shape
    Lc = pc.shape[1]
    assert L == NA_ROWS * GRID_W
    tq = NA_QT * GRID_W
    return pl.pallas_call(
        _na_body,
        grid=(B, NA_HEADS // 2, L // tq),
        in_specs=[pl.BlockSpec((1, tq, LANE), lambda b, h, t: (b, t, NA_Q + h)),
                  pl.BlockSpec((1, L, LANE), lambda b, h, t: (b, 0, NA_K + h)),
                  pl.BlockSpec((1, L, LANE), lambda b, h, t: (b, 0, NA_V + h)),
                  pl.BlockSpec((1, Lc, LANE), lambda b, h, t: (b, 0, NA_K + h)),
                  pl.BlockSpec((1, Lc, LANE), lambda b, h, t: (b, 0, NA_V + h)),
                  pl.BlockSpec((1, 2, tq, NA_BAND * GRID_W), lambda b, h, t: (t, h, 0, 0))],
        out_specs=pl.BlockSpec((1, tq, LANE), lambda b, h, t: (b, t, h)),
        out_shape=jax.ShapeDtypeStruct((B, L, GROUP_W), F32),
        compiler_params=_params(("arbitrary", "arbitrary", "arbitrary"), 40),
        name="na_attention",
    )(p, p, p, pc, pc, bias)


def _ctx_attn_body(q_ref, k_ref, v_ref, o_ref):
    q = q_ref[0] * (NA_HEAD_DIM ** -0.5)
    o_ref[0] = _masked_heads_attention(q, [k_ref[0].astype(BF16)], [v_ref[0].astype(BF16)], [None])


def _ctx_attention(pc):
    B, Lc, _ = pc.shape
    return pl.pallas_call(
        _ctx_attn_body,
        grid=(B, NA_HEADS // 2),
        in_specs=[pl.BlockSpec((1, Lc, LANE), lambda b, h: (b, 0, NA_Q + h)),
                  pl.BlockSpec((1, Lc, LANE), lambda b, h: (b, 0, NA_K + h)),
                  pl.BlockSpec((1, Lc, LANE), lambda b, h: (b, 0, NA_V + h))],
        out_specs=pl.BlockSpec((1, Lc, LANE), lambda b, h: (b, 0, h)),
        out_shape=jax.ShapeDtypeStruct((B, Lc, GROUP_W), F32),
        compiler_params=_params(("arbitrary", "arbitrary"), 32),
        name="ctx_attention",
    )(pc, pc, pc)


CONV_TT = 256
HALO = 8


def _conv3_body(xp_ref, x_ref, xn_ref, w_ref, b_ref, o_ref, *, silu, nt):
    i = pl.program_id(1)
    x = x_ref[0]
    tt = x.shape[0]
    prev_row = jnp.where(i == 0, 0.0, xp_ref[0, HALO - 1:HALO, :])
    next_row = jnp.where(i == nt - 1, 0.0, xn_ref[0, 0:1, :])
    row = lax.broadcasted_iota(I32, x.shape, 0)
    xm = jnp.where(row == 0, prev_row, pltpu.roll(x, 1, 0))
    xp = jnp.where(row == tt - 1, next_row, pltpu.roll(x, tt - 1, 0))
    y = xm * w_ref[0:1, :] + x * w_ref[1:2, :] + xp * w_ref[2:3, :] + b_ref[...]
    o_ref[0] = y * _sigmoid(y) if silu else y


def _conv3(p, col0, w, b, silu):
    B, L, _ = p.shape
    C = w.shape[1]
    cw = 512
    tt = min(CONV_TT, L)
    nt = L // tt
    hb = tt // HALO
    cb = col0 * LANE // cw
    assert col0 * LANE % cw == 0
    return pl.pallas_call(
        functools.partial(_conv3_body, silu=silu, nt=nt),
        grid=(B, nt, C // cw),
        in_specs=[pl.BlockSpec((1, HALO, cw), lambda bb, i, j: (bb, jnp.maximum(i * hb - 1, 0), cb + j)),
                  pl.BlockSpec((1, tt, cw), lambda bb, i, j: (bb, i, cb + j)),
                  pl.BlockSpec((1, HALO, cw), lambda bb, i, j: (bb, jnp.minimum((i + 1) * hb, L // HALO - 1), cb + j)),
                  pl.BlockSpec((3, cw), lambda bb, i, j: (0, j)),
                  pl.BlockSpec((1, cw), lambda bb, i, j: (0, j))],
        out_specs=pl.BlockSpec((1, tt, cw), lambda bb, i, j: (bb, i, j)),
        out_shape=jax.ShapeDtypeStruct((B, L, C), F32),
        compiler_params=_params(("arbitrary", "arbitrary", "arbitrary"), 32),
        name="conv3",
    )(p, p, p, w, b)


def _dft_tables(L):
    N = 2 * L
    f = jnp.arange(L, dtype=I32)[:, None]
    t = jnp.arange(L, dtype=I32)[None, :]
    ang = ((f * t) % N).astype(F32) * (2.0 * math.pi / N)
    c, s = jnp.cos(ang), jnp.sin(ang)
    nyq = jnp.where(t % 2 == 0, 1.0, -1.0).astype(F32)
    dc = f == 0
    fc = c
    fs = jnp.where(dc, nyq, -s)
    ci = (jnp.where(dc, 1.0, 2.0) * c / N).T
    si = (jnp.where(dc, nyq, -2.0 * s) / N).T
    return fc.astype(BF16), fs.astype(BF16), ci.astype(BF16), si.astype(BF16)


def _hy_filter_body(w1_ref, b1_ref, w2_ref, b2_ref, w3_ref, b3_ref, win_ref, o_ref, *, L, tt):
    i = pl.program_id(0)
    pos = (lax.broadcasted_iota(I32, (tt, LANE), 0) + i * tt).astype(F32) / L
    lane = lax.broadcasted_iota(I32, (tt, LANE), 1)
    band = jnp.where(lane <= HY_BANDS, lane, lane - HY_BANDS).astype(F32)
    ang = (2.0 * math.pi) * pos * band
    feats = jnp.where(lane == 0, pos, jnp.where(lane <= HY_BANDS, jnp.sin(ang),
                                                jnp.where(lane <= 2 * HY_BANDS, jnp.cos(ang), 0.0)))
    h = jnp.sin(HY_SIN_FREQ * (jnp.dot(feats, w1_ref[...], preferred_element_type=F32, precision=HI) + b1_ref[...]))
    h = jnp.sin(HY_SIN_FREQ * (jnp.dot(h, w2_ref[...], preferred_element_type=F32, precision=HI) + b2_ref[...]))
    h = jnp.dot(h, w3_ref[...], preferred_element_type=F32, precision=HI) + b3_ref[...]
    t1 = (lax.broadcasted_iota(I32, (tt, 1), 0) + i * tt).astype(F32) / L
    h = h * jnp.exp(-t1 * win_ref[...])
    row = lax.broadcasted_iota(I32, h.shape, 0) + i * tt
    col = lax.broadcasted_iota(I32, h.shape, 1)
    bwd = (col // HY_CH) % 2 == 1
    o_ref[...] = jnp.where(jnp.logical_and(row == 0, bwd), 0.0, h)


def _hy_filters(L, w1, b1, w2, b2, w3, b3):
    tt = min(256, L)
    nc = 4 * HY_CH
    w1p = jnp.zeros((LANE, HY_HIDDEN), F32).at[:w1.shape[0]].set(w1)
    max_decay = math.log(HY_DECAY_TARGET) / HY_FAST_DECAY
    min_decay = math.log(HY_DECAY_TARGET) / HY_SLOW_DECAY
    deltas = jnp.abs(jnp.linspace(min_decay, max_decay, HY_CH, dtype=F32))
    win = jnp.tile(deltas, 4)[None, :]
    fix = lambda i: (0, 0)
    return pl.pallas_call(
        functools.partial(_hy_filter_body, L=L, tt=tt),
        grid=(L // tt,),
        in_specs=[pl.BlockSpec((LANE, HY_HIDDEN), fix), pl.BlockSpec((1, HY_HIDDEN), fix),
                  pl.BlockSpec((HY_HIDDEN, HY_HIDDEN), fix), pl.BlockSpec((1, HY_HIDDEN), fix),
                  pl.BlockSpec((HY_HIDDEN, nc), fix), pl.BlockSpec((1, nc), fix), pl.BlockSpec((1, nc), fix)],
        out_specs=pl.BlockSpec((tt, nc), lambda i: (i, 0)),
        out_shape=jax.ShapeDtypeStruct((L, nc), F32),
        compiler_params=_params(("arbitrary",), 32),
        name="hyena_filters",
    )(w1p, b1[None, :], w2, b2[None, :], w3, b3[None, :], win)


def _hy_spectrum_body(fc_ref, fs_ref, h_ref, gc_ref, gs_ref):
    hb = h_ref[...].astype(BF16)
    gc_ref[...] = jnp.dot(fc_ref[...], hb, preferred_element_type=F32)
    gs_ref[...] = jnp.dot(fs_ref[...], hb, preferred_element_type=F32)


def _hy_spectrum(fc, fs, hmat):
    L, nc = hmat.shape
    tf = min(512, L)
    tn = 512
    return pl.pallas_call(
        _hy_spectrum_body,
        grid=(nc // tn, L // tf),
        in_specs=[pl.BlockSpec((tf, L), lambda j, i: (i, 0)), pl.BlockSpec((tf, L), lambda j, i: (i, 0)),
                  pl.BlockSpec((L, tn), lambda j, i: (0, j))],
        out_specs=[pl.BlockSpec((tf, tn), lambda j, i: (i, j)), pl.BlockSpec((tf, tn), lambda j, i: (i, j))],
        out_shape=[jax.ShapeDtypeStruct((L, nc), F32), jax.ShapeDtypeStruct((L, nc), F32)],
        compiler_params=_params(("arbitrary", "arbitrary"), 40),
        name="hyena_spectrum",
    )(fc, fs, hmat)


def _hy_fwd_body(fc_ref, fs_ref, z_ref, gc_ref, gs_ref, yre_ref, yim_ref, zb):
    i = pl.program_id(1)

    @pl.when(i == 0)
    def _():
        zb[...] = z_ref[0].astype(BF16)

    C = HY_CH
    zre = jnp.dot(fc_ref[...], zb[...], preferred_element_type=F32)
    zim = jnp.dot(fs_ref[...], zb[...], preferred_element_type=F32)
    gre = gc_ref[:, :C] + gc_ref[:, C:]
    a = gs_ref[:, :C]
    b = gs_ref[:, C:]
    packed = jnp.logical_and(lax.broadcasted_iota(I32, zre.shape, 0) == 0, i == 0)
    gim = jnp.where(packed, a + b, a - b)
    yre_ref[0] = jnp.where(packed, zre * gre, zre * gre - zim * gim).astype(BF16)
    yim_ref[0] = jnp.where(packed, zim * gim, zre * gim + zim * gre).astype(BF16)


def _hy_forward(fc, fs, z_src, z_col, gc, gs, order):
    B, L, _ = z_src.shape
    C = HY_CH
    tf = min(512, L)
    return pl.pallas_call(
        _hy_fwd_body,
        grid=(B, L // tf),
        in_specs=[pl.BlockSpec((tf, L), lambda b, i: (i, 0)), pl.BlockSpec((tf, L), lambda b, i: (i, 0)),
                  pl.BlockSpec((1, L, C), lambda b, i: (b, 0, z_col)),
                  pl.BlockSpec((tf, 2 * C), lambda b, i: (i, order)), pl.BlockSpec((tf, 2 * C), lambda b, i: (i, order))],
        out_specs=[pl.BlockSpec((1, tf, C), lambda b, i: (b, i, 0)), pl.BlockSpec((1, tf, C), lambda b, i: (b, i, 0))],
        out_shape=[jax.ShapeDtypeStruct((B, L, C), BF16), jax.ShapeDtypeStruct((B, L, C), BF16)],
        scratch_shapes=[pltpu.VMEM((L, C), BF16)],
        compiler_params=_params(("arbitrary", "arbitrary"), 40),
        name="hyena_forward",
    )(fc, fs, z_src, gc, gs)


def _hy_inv_body(ci_ref, si_ref, yre_ref, yim_ref, z_ref, x_ref, skip_ref, o_ref):
    y = jnp.dot(ci_ref[...], yre_ref[0], preferred_element_type=F32)
    y += jnp.dot(si_ref[...], yim_ref[0], preferred_element_type=F32)
    o_ref[0] = x_ref[0] * (y + z_ref[0] * skip_ref[0])


def _hy_inverse(ci, si, yre, yim, z_src, z_col, x_src, x_col, skip, order):
    B, L, C = yre.shape
    tt = min(512, L)
    return pl.pallas_call(
        _hy_inv_body,
        grid=(B, L // tt),
        in_specs=[pl.BlockSpec((tt, L), lambda b, i: (i, 0)), pl.BlockSpec((tt, L), lambda b, i: (i, 0)),
                  pl.BlockSpec((1, L, C), lambda b, i: (b, 0, 0)), pl.BlockSpec((1, L, C), lambda b, i: (b, 0, 0)),
                  pl.BlockSpec((1, tt, C), lambda b, i: (b, i, z_col)),
                  pl.BlockSpec((1, tt, C), lambda b, i: (b, i, x_col)),
                  pl.BlockSpec((1, 1, C), lambda b, i: (order, 0, 0))],
        out_specs=pl.BlockSpec((1, tt, C), lambda b, i: (b, i, 0)),
        out_shape=jax.ShapeDtypeStruct((B, L, C), F32),
        compiler_params=_params(("arbitrary", "arbitrary"), 40),
        name="hyena_inverse",
    )(ci, si, yre, yim, z_src, x_src, skip)


def _hyena(p, tables, conv_w, conv_b, w1, b1, w2, b2, w3, b3, skip):
    L = p.shape[1]
    u = _conv3(p, HY_U, conv_w, conv_b[None, :], silu=False)
    fc, fs, ci, si = tables
    gc, gs = _hy_spectrum(fc, fs, _hy_filters(L, w1, b1, w2, b2, w3, b3))
    skip3 = skip[:, None, :]
    yre, yim = _hy_forward(fc, fs, u, 2, gc, gs, 0)
    z1 = _hy_inverse(ci, si, yre, yim, u, 2, u, 0, skip3, 0)
    yre, yim = _hy_forward(fc, fs, z1, 0, gc, gs, 1)
    return _hy_inverse(ci, si, yre, yim, z1, 0, u, 1, skip3, 1)


REC_HP = 4
REC_UNROLL = 1
NT_DIMS = (((1,), (1,)), ((), ()))
TN_DIMS = (((0,), (0,)), ((), ()))


def _chunk_masks(reverse):
    r = lax.broadcasted_iota(I32, (CHUNK, CHUNK), 0)
    c = lax.broadcasted_iota(I32, (CHUNK, CHUNK), 1)
    if reverse:
        return (r >= c).astype(F32), c >= r, c > r, r == c
    return (r <= c).astype(F32), c <= r, c < r, r == c


def _to_col(row, eye):
    return jnp.sum(jnp.where(eye, row, 0.0), axis=1, keepdims=True)


def _log_sigmoid(x):
    return jnp.minimum(x, 0.0) - jnp.log1p(jnp.exp(-jnp.abs(x)))


def _softplus(x):
    return jnp.maximum(x, 0.0) + jnp.log1p(jnp.exp(-jnp.abs(x)))


def _bdot(a, b):
    return jnp.dot(a.astype(BF16), b.astype(BF16), preferred_element_type=F32)


def _split_bf16(a):
    hi = a.astype(BF16)
    return hi, (a - hi.astype(F32)).astype(BF16)


def _dot3(a, b):
    d = lambda x, y: jnp.dot(x, y, preferred_element_type=F32)
    return d(a[0], b[0]) + (d(a[0], b[1]) + d(a[1], b[0]))


def _bdot_g(a, b, dims):
    return lax.dot_general(a.astype(BF16), b.astype(BF16), dims, preferred_element_type=F32)


def _ml_steps(items, c_refs, n_refs, m_refs):
    n = range(len(items))
    q, k, v, i_row, f_row, rev = zip(*items)
    masks = [_chunk_masks(r) for r in rev]
    incl, mask, _, eye = zip(*masks)
    b_row = [jnp.dot(f_row[i], incl[i], preferred_element_type=F32, precision=HI) for i in n]
    qk = [_bdot_g(q[i], k[i], NT_DIMS) for i in n]
    cst = [c_refs[i][...] for i in n]
    nst = [n_refs[i][...] for i in n]
    m = [m_refs[i][...] for i in n]
    qc = [_bdot(q[i], cst[i]) for i in n]
    b_col = [_to_col(b_row[i], eye[i]) for i in n]
    gi_row = [i_row[i] - b_row[i] for i in n]
    dlog = [jnp.where(mask[i], b_col[i] + gi_row[i], -jnp.inf) for i in n]
    a_col = [b_col[i] + m[i] for i in n]
    mt = [jnp.maximum(a_col[i], jnp.max(dlog[i], axis=1, keepdims=True)) for i in n]
    s = [qk[i] * jnp.exp(dlog[i] - mt[i]) for i in n]
    w_inter = [jnp.exp(a_col[i] - mt[i]) for i in n]
    sv = [_bdot(s[i], v[i]) for i in n]
    b_last = [b_row[i][:, 0:1] if rev[i] else b_row[i][:, CHUNK - 1:CHUNK] for i in n]
    g_row = [b_last[i] + gi_row[i] for i in n]
    m_new = [jnp.maximum(b_last[i] + m[i], jnp.max(g_row[i], axis=1, keepdims=True)) for i in n]
    kw = [k[i] * _to_col(jnp.exp(g_row[i] - m_new[i]), eye[i]) for i in n]
    kv = [_bdot_g(kw[i], v[i], TN_DIMS) for i in n]
    decay = [jnp.exp(b_last[i] + m[i] - m_new[i]) for i in n]
    hs = []
    for i in n:
        num = sv[i] + w_inter[i] * qc[i]
        den = jnp.sum(s[i], axis=1, keepdims=True) + w_inter[i] * jnp.sum(q[i] * nst[i], axis=1, keepdims=True)
        hs.append(num / jnp.maximum(jnp.abs(den), jnp.exp(-mt[i])))
        c_refs[i][...] = decay[i] * cst[i] + kv[i]
        n_refs[i][...] = decay[i] * nst[i] + jnp.sum(kw[i], axis=0, keepdims=True)
        m_refs[i][...] = m_new[i]
    return hs


def _mlstm_body(ql_ref, kl_ref, vl_ref, ol_ref, qc_ref, kc_ref, vc_ref, oc_ref, gl_ref, gc_ref, gb_ref,
                cos_ref, sin_ref, ng_ref, *rest, need_ctx):
    nchain = 2 * REC_HP
    n_out = 2 if need_ctx else 1
    y_ref = rest[0]
    yc_ref = rest[1] if need_ctx else None
    c_scr = rest[n_out:n_out + nchain]
    n_scr = rest[n_out + nchain:n_out + 2 * nchain]
    m_scr = rest[n_out + 2 * nchain:n_out + 3 * nchain]
    hl = rest[n_out + 3 * nchain]
    hc = rest[n_out + 3 * nchain + 1] if need_ctx else None
    for ref in (*c_scr, *n_scr, *m_scr):
        ref[...] = jnp.zeros_like(ref)

    def run(q_ref, k_ref, v_ref, g_ref, h_scr, rope):
        nc = g_ref.shape[3]

        def body(j, carry):
            items, where = [], []
            for hh in range(REC_HP):
                lanes = slice(hh * LANE, (hh + 1) * LANE)
                for d in range(2):
                    jj = nc - 1 - j if d else j
                    off = pl.multiple_of(jj * CHUNK, CHUNK)
                    q = q_ref[0, pl.ds(off, CHUNK), lanes]
                    k = k_ref[0, pl.ds(off, CHUNK), lanes]
                    v = v_ref[0, pl.ds(off, CHUNK), lanes]
                    if rope:
                        cs = cos_ref[pl.ds(off, CHUNK), :]
                        sn = sin_ref[pl.ds(off, CHUNK), :]
                        q = q * cs + pltpu.roll(q, ML_HEAD_DIM // 2, 1) * sn
                        k = k * cs + pltpu.roll(k, ML_HEAD_DIM // 2, 1) * sn
                    q = q * (ML_HEAD_DIM ** -0.5)
                    i_row = g_ref[0, hh, 2 * d, pl.ds(jj, 1), :] + gb_ref[hh, 2 * d]
                    f_row = _log_sigmoid(g_ref[0, hh, 2 * d + 1, pl.ds(jj, 1), :] + gb_ref[hh, 2 * d + 1])
                    items.append((q, k, v, i_row, f_row, bool(d)))
                    where.append((d, off, lanes))
            hs = _ml_steps(items, c_scr, n_scr, m_scr)
            if h_scr is not None:
                for h, (d, off, lanes) in zip(hs, where):
                    h_scr[d, pl.ds(off, CHUNK), lanes] = h
            return carry

        lax.fori_loop(0, nc, body, 0, unroll=REC_UNROLL)

    def finish(h_scr, o_ref, out_ref):
        for hh in range(REC_HP):
            lanes = slice(hh * LANE, (hh + 1) * LANE)
            h = h_scr[0, :, lanes] + h_scr[1, :, lanes]
            dev = h - jnp.mean(h, axis=-1, keepdims=True)
            var = jnp.mean(dev * dev, axis=-1, keepdims=True)
            out_ref[0, :, lanes] = dev * lax.rsqrt(var + EPS) * ng_ref[:, lanes] * _sigmoid(o_ref[0, :, lanes])

    run(qc_ref, kc_ref, vc_ref, gc_ref, hc, False)
    run(ql_ref, kl_ref, vl_ref, gl_ref, hl, True)
    finish(hl, ol_ref, y_ref)
    if need_ctx:
        finish(hc, oc_ref, yc_ref)


def _rope_tables(L):
    n_freq = ML_HEAD_DIM // 4
    t = jnp.arange(L)
    row = (t // GRID_W).astype(F32)
    col = (t % GRID_W).astype(F32)
    inv = ROPE_BASE ** (-jnp.arange(n_freq, dtype=F32) / n_freq)
    ang = jnp.concatenate([row[:, None] * inv, col[:, None] * inv], axis=-1)
    c, s = jnp.cos(ang), jnp.sin(ang)
    return jnp.concatenate([c, c], axis=-1), jnp.concatenate([-s, s], axis=-1)


def _gate_rows(g, n_types, heads):
    B, L, _ = g.shape
    return g.reshape(B, L, n_types, heads).transpose(0, 3, 2, 1).reshape(B, heads, n_types, L // CHUNK, CHUNK)


def _mlstm(p, pc, rope, gate_b, norm_g, need_ctx):
    B, L, _ = p.shape
    Lc = pc.shape[1]
    H = ML_HEADS
    gl = _gate_rows(p[:, :, SMALL * LANE:SMALL * LANE + 4 * H], 4, H)
    gc = _gate_rows(pc[:, :, SMALL * LANE:SMALL * LANE + 4 * H], 4, H)
    gb = gate_b.T.reshape(H, 4, 1, 1)
    cos, sin = rope
    W = REC_HP * LANE
    nchain = 2 * REC_HP
    lat = lambda col: pl.BlockSpec((1, L, W), lambda b, h: (b, 0, col // REC_HP + h), pipeline_mode=pl.Buffered(1))
    ctx = lambda col: pl.BlockSpec((1, Lc, W), lambda b, h: (b, 0, col // REC_HP + h))
    gspec = lambda n: pl.BlockSpec((1, REC_HP, 4, n // CHUNK, CHUNK), lambda b, h: (b, h, 0, 0, 0))
    fix = lambda b, h: (0, 0)
    out_specs = [pl.BlockSpec((1, L, W), lambda b, h: (b, 0, h))]
    out_shape = [jax.ShapeDtypeStruct((B, L, GROUP_W), F32)]
    scratch = ([pltpu.VMEM((ML_HEAD_DIM, ML_HEAD_DIM), F32)] * nchain + [pltpu.VMEM((1, ML_HEAD_DIM), F32)] * nchain
               + [pltpu.VMEM((1, 1), F32)] * nchain + [pltpu.VMEM((2, L, W), F32)])
    if need_ctx:
        out_specs.append(pl.BlockSpec((1, Lc, W), lambda b, h: (b, 0, h)))
        out_shape.append(jax.ShapeDtypeStruct((B, Lc, GROUP_W), F32))
        scratch.append(pltpu.VMEM((2, Lc, W), F32))
    res = pl.pallas_call(
        functools.partial(_mlstm_body, need_ctx=need_ctx),
        grid=(B, H // REC_HP),
        in_specs=[lat(ML_Q), lat(ML_K), lat(ML_V), lat(ML_O), ctx(ML_Q), ctx(ML_K), ctx(ML_V), ctx(ML_O),
                  gspec(L), gspec(Lc), pl.BlockSpec((REC_HP, 4, 1, 1), lambda b, h: (h, 0, 0, 0)),
                  pl.BlockSpec((L, LANE), fix), pl.BlockSpec((L, LANE), fix),
                  pl.BlockSpec((1, W), lambda b, h: (0, h))],
        out_specs=out_specs,
        out_shape=out_shape,
        scratch_shapes=scratch,
        compiler_params=_params(("arbitrary", "arbitrary"), 52),
        name="mlstm",
    )(p, p, p, p, pc, pc, pc, pc, gl, gc, gb, cos, sin, norm_g[None, :])
    return (res[0], res[1]) if need_ctx else (res[0], None)


def _dn_prep(items):
    n = range(len(items))
    q, k, v, a_row, beta_row, rev = zip(*items)
    masks = [_chunk_masks(r) for r in rev]
    incl, mask, strict, eye = zip(*masks)
    g_row = [jnp.dot(a_row[i], incl[i], preferred_element_type=F32, precision=HI) for i in n]
    g_col = [_to_col(g_row[i], eye[i]) for i in n]
    beta_col = [_to_col(beta_row[i], eye[i]) for i in n]
    dec = [jnp.exp(jnp.where(mask[i], g_col[i] - g_row[i], -jnp.inf)) for i in n]
    kbeta = [k[i] * beta_col[i] for i in n]
    eg_col = [jnp.exp(g_col[i]) for i in n]
    kk = [_bdot_g(kbeta[i], k[i], NT_DIMS) for i in n]
    x = [-jnp.where(strict[i], kk[i] * dec[i], 0.0) for i in n]
    inv = [eye[i].astype(F32) + x[i] for i in n]
    xs = [_split_bf16(x[i]) for i in n]
    for _ in range(5):
        x = [_dot3(xs[i], xs[i]) for i in n]
        xs = [_split_bf16(x[i]) for i in n]
        invs = [_split_bf16(inv[i]) for i in n]
        inv = [inv[i] + _dot3(invs[i], xs[i]) for i in n]
    invs = [_split_bf16(inv[i]) for i in n]
    u = [_dot3(invs[i], _split_bf16(v[i] * beta_col[i])) for i in n]
    w = [_dot3(invs[i], _split_bf16(kbeta[i] * eg_col[i])) for i in n]
    attn = [_bdot_g(q[i], k[i], NT_DIMS) * dec[i] for i in n]
    g_last = [g_row[i][:, 0:1] if rev[i] else g_row[i][:, CHUNK - 1:CHUNK] for i in n]
    qe = [q[i] * eg_col[i] for i in n]
    kd = [k[i] * jnp.exp(g_last[i] - g_col[i]) for i in n]
    return [(u[i], w[i], attn[i], qe[i], kd[i], jnp.exp(g_last[i])) for i in n]


def _dn_apply(preps, s_refs):
    n = range(len(preps))
    u, w, attn, qe, kd, eg_last = zip(*preps)
    st = [s_refs[i][...] for i in n]
    ws = [_bdot(w[i], st[i]) for i in n]
    qs = [_bdot(qe[i], st[i]) for i in n]
    v_new = [u[i] - ws[i] for i in n]
    av = [_bdot(attn[i], v_new[i]) for i in n]
    kv = [_bdot_g(kd[i], v_new[i], TN_DIMS) for i in n]
    for i in n:
        s_refs[i][...] = eg_last[i] * st[i] + kv[i]
    return [qs[i] + av[i] for i in n]


def _deltanet_body(ql_ref, kl_ref, vl_ref, zl_ref, qc_ref, kc_ref, vc_ref, zc_ref, gl_ref, gc_ref, prm_ref,
                   ng_ref, *rest, need_ctx):
    nchain = 2 * REC_HP
    n_out = 2 if need_ctx else 1
    y_ref = rest[0]
    yc_ref = rest[1] if need_ctx else None
    s_scr = rest[n_out:n_out + nchain]
    ol = rest[n_out + nchain]
    oc = rest[n_out + nchain + 1] if need_ctx else None
    for ref in s_scr:
        ref[...] = jnp.zeros_like(ref)

    def unit(x):
        return x * lax.rsqrt(jnp.sum(x * x, axis=-1, keepdims=True) + EPS)

    def run(q_ref, k_ref, v_ref, g_ref, o_scr):
        nc = g_ref.shape[3]

        chains = [(hh, d) for hh in range(REC_HP) for d in range(2)]

        def body(j2, carry):
            items, where = [], []
            for t in range(REC_UNROLL):
                j = j2 * REC_UNROLL + t
                for hh, d in chains:
                    lanes = slice(hh * LANE, (hh + 1) * LANE)
                    jj = nc - 1 - j if d else j
                    off = pl.multiple_of(jj * CHUNK, CHUNK)
                    q = unit(q_ref[0, pl.ds(off, CHUNK), lanes]) * (DN_HEAD_DIM ** -0.5)
                    k = unit(k_ref[0, pl.ds(off, CHUNK), lanes])
                    v = v_ref[0, pl.ds(off, CHUNK), lanes]
                    a_raw = g_ref[0, hh, 2 * d, pl.ds(jj, 1), :]
                    a_row = -jnp.exp(prm_ref[hh, 2 * d]) * _softplus(a_raw + prm_ref[hh, 2 * d + 1])
                    beta_row = _sigmoid(g_ref[0, hh, 2 * d + 1, pl.ds(jj, 1), :])
                    items.append((q, k, v, a_row, beta_row, bool(d)))
                    where.append((d, off, lanes))
            preps = _dn_prep(items)
            nch = len(chains)
            for t in range(REC_UNROLL):
                outs = _dn_apply(preps[t * nch:(t + 1) * nch], s_scr)
                if o_scr is not None:
                    for o, (d, off, lanes) in zip(outs, where[t * nch:(t + 1) * nch]):
                        o_scr[d, pl.ds(off, CHUNK), lanes] = o
            return carry

        lax.fori_loop(0, nc // REC_UNROLL, body, 0)

    def finish(o_scr, z_ref, out_ref):
        for hh in range(REC_HP):
            lanes = slice(hh * LANE, (hh + 1) * LANE)
            o = o_scr[0, :, lanes] + o_scr[1, :, lanes]
            z = z_ref[0, :, lanes]
            out_ref[0, :, lanes] = o * lax.rsqrt(_mean_sq(o) + EPS) * ng_ref[...] * (z * _sigmoid(z))

    run(qc_ref, kc_ref, vc_ref, gc_ref, oc)
    run(ql_ref, kl_ref, vl_ref, gl_ref, ol)
    finish(ol, zl_ref, y_ref)
    if need_ctx:
        finish(oc, zc_ref, yc_ref)


def _deltanet(p, pc, conv_w, a_log, dt_bias, norm_g, need_ctx):
    B, L, _ = p.shape
    Lc = pc.shape[1]
    H = DN_HEADS
    zero_b = jnp.zeros((1, 3 * GROUP_W), F32)
    u = _conv3(p, DN_Q, conv_w, zero_b, silu=True)
    uc = _conv3(pc, DN_Q, conv_w, zero_b, silu=True)

    def gates(src):
        base = SMALL * LANE + 4 * ML_HEADS
        beta = src[:, :, base:base + 2 * H].reshape(src.shape[0], src.shape[1], 2, 1, H)
        a = src[:, :, base + 2 * H:base + 4 * H].reshape(src.shape[0], src.shape[1], 2, 1, H)
        both = jnp.concatenate([a, beta], axis=3).reshape(src.shape[0], src.shape[1], 4 * H)
        return _gate_rows(both, 4, H)

    prm = jnp.stack([a_log[0], dt_bias[0], a_log[1], dt_bias[1]], axis=1).reshape(H, 4, 1, 1)
    W = REC_HP * LANE
    lat = lambda blk: pl.BlockSpec((1, L, W), lambda b, h: (b, 0, blk // REC_HP + h), pipeline_mode=pl.Buffered(1))
    ctx = lambda blk: pl.BlockSpec((1, Lc, W), lambda b, h: (b, 0, blk // REC_HP + h))
    gspec = lambda n: pl.BlockSpec((1, REC_HP, 4, n // CHUNK, CHUNK), lambda b, h: (b, h, 0, 0, 0))
    out_specs = [pl.BlockSpec((1, L, W), lambda b, h: (b, 0, h))]
    out_shape = [jax.ShapeDtypeStruct((B, L, GROUP_W), F32)]
    scratch = [pltpu.VMEM((DN_HEAD_DIM, DN_HEAD_DIM), F32)] * (2 * REC_HP) + [pltpu.VMEM((2, L, W), F32)]
    if need_ctx:
        out_specs.append(pl.BlockSpec((1, Lc, W), lambda b, h: (b, 0, h)))
        out_shape.append(jax.ShapeDtypeStruct((B, Lc, GROUP_W), F32))
        scratch.append(pltpu.VMEM((2, Lc, W), F32))
    res = pl.pallas_call(
        functools.partial(_deltanet_body, need_ctx=need_ctx),
        grid=(B, H // REC_HP),
        in_specs=[lat(0), lat(4), lat(8), lat(DN_G), ctx(0), ctx(4), ctx(8), ctx(DN_G),
                  gspec(L), gspec(Lc), pl.BlockSpec((REC_HP, 4, 1, 1), lambda b, h: (h, 0, 0, 0)),
                  pl.BlockSpec((1, LANE), lambda b, h: (0, 0))],
        out_specs=out_specs,
        out_shape=out_shape,
        scratch_shapes=scratch,
        compiler_params=_params(("arbitrary", "arbitrary"), 52),
        name="deltanet",
    )(u, u, u, p, uc, uc, uc, pc, gates(p), gates(pc), prm, norm_g[None, :])
    return (res[0], res[1]) if need_ctx else (res[0], None)


def _permute_w_in(w):
    D = w.shape[0]
    G = GROUP_W
    n_gate = 4 * ML_HEADS
    o = 0
    na, o = w[:, o:o + 3 * G], o + 3 * G
    ml_qkv, o = w[:, o:o + 3 * G], o + 3 * G
    ml_o, o = w[:, o:o + G], o + G
    ml_gates, o = w[:, o:o + n_gate], o + n_gate
    hy, o = w[:, o:o + 3 * G], o + 3 * G
    dn_qkv, o = w[:, o:o + 3 * G], o + 3 * G
    dn_gate, o = w[:, o:o + G], o + G
    dn_small = w[:, o:o + 4 * DN_HEADS]

    def split_pairs(m):
        return m.reshape(D, ML_HEADS, ML_HEAD_DIM // 2, 2).transpose(0, 1, 3, 2).reshape(D, G)

    used = 56 * LANE + n_gate + 4 * DN_HEADS
    parts = [na, split_pairs(ml_qkv[:, :G]), split_pairs(ml_qkv[:, G:2 * G]), ml_qkv[:, 2 * G:], ml_o, hy, dn_qkv,
             dn_gate, ml_gates, dn_small, jnp.zeros((D, P_COLS - used), w.dtype)]
    return jnp.concatenate(parts, axis=1).astype(BF16)


def kernel(x, c, ctx, c_ctx, w_ada, b_ada, norm_g, w_in, w_out, na_rpb, ml_gate_b, ml_norm_g, hy_conv_w, hy_conv_b, hy_f_w1, hy_f_b1, hy_f_w2, hy_f_b2, hy_f_w3, hy_f_b3, hy_skip, dn_conv_w, dn_a_log, dn_dt_bias, dn_norm_g, router_w, router_b, moe_w_gu, moe_b_gu, moe_w_down, moe_b_down):
    B, L, D = x.shape
    Lc = ctx.shape[1]
    assert D == D_MODEL and B < 16
    cc = jnp.zeros((16, D), F32).at[:B].set(c).at[B].set(c_ctx)
    mods = _adaln(cc, w_ada, b_ada)
    x2 = x.reshape(B * L, D)
    xc2 = ctx.reshape(B * Lc, D)
    flat = lambda y: y.reshape(-1, GROUP_W)
    rope = _rope_tables(L)
    dft_l = _dft_tables(L)
    dft_c = _dft_tables(Lc)
    for l in range(DEPTH):
        need_ctx = l < DEPTH - 1
        mod = mods[l, :B].reshape(B, 6, 1, D)
        mod_c = mods[l, B].reshape(1, 6, 1, D)
        g = norm_g[l]
        w_p = _permute_w_in(w_in[l])
        p = _inproj(x2, L, mod, g[0:1], w_p).reshape(B, L, P_COLS)
        pc = _inproj(xc2, Lc, mod_c, g[0:1], w_p).reshape(B, Lc, P_COLS)
        hy = (hy_conv_w[l], hy_conv_b[l], hy_f_w1[l], hy_f_b1[l], hy_f_w2[l], hy_f_b2[l], hy_f_w3[l], hy_f_b3[l],
              hy_skip[l])
        ya = _na_attention(p, pc, _na_bias_table(na_rpb[l]))
        yb, ybc = _mlstm(p, pc, rope, ml_gate_b[l], ml_norm_g[l], need_ctx)
        yh = _hyena(p, dft_l, *hy)
        yd, ydc = _deltanet(p, pc, dn_conv_w[l], dn_a_log[l], dn_dt_bias[l], dn_norm_g[l], need_ctx)
        w_out_bf = w_out[l].astype(BF16)
        rb = router_b[l][None, :]
        x2, tok, logits = _outproj([flat(ya), flat(yb), flat(yh), flat(yd)], w_out_bf, x2, L, mod, g[1:2], g[2:3],
                                   router_w[l], rb)
        if need_ctx:
            yac = _ctx_attention(pc)
            yhc = _hyena(pc, dft_c, *hy)
            xc2, tok_c, logits_c = _outproj([flat(yac), flat(ybc), flat(yhc), flat(ydc)], w_out_bf, xc2, Lc, mod_c,
                                            g[1:2], g[2:3], router_w[l], rb)
            tok = jnp.concatenate([tok, tok_c], axis=0)
            logits = jnp.concatenate([logits, logits_c], axis=0)
        dest3, gates, ys = _moe(tok, logits, moe_w_gu, moe_b_gu, moe_w_down, moe_b_down, l)
        nl = B * L // MOE_TM
        x2 = _combine(dest3[:nl], gates[:B * L], ys, x2, L, mod, g[3:4])
        if need_ctx:
            xc2 = _combine(dest3[nl:], gates[B * L:], ys, xc2, Lc, mod_c, g[3:4])
    return x2.reshape(B, L, D)
```

```python
import functools
import math

import numpy as np
import jax
import jax.numpy as jnp
from jax import lax
from jax.experimental import pallas as pl
from jax.experimental.pallas import tpu as pltpu

F32 = jnp.float32
BF16 = jnp.bfloat16
I32 = jnp.int32

D_MODEL = 2048
DEPTH = 2
GRID_W = 64
N_MIXERS = 4
GROUP_W = D_MODEL // N_MIXERS
NA_HEADS = 8
NA_HEAD_DIM = GROUP_W // NA_HEADS
NA_WIN_H = 8
NA_WIN_W = 16
ML_HEADS = 4
ML_HEAD_DIM = GROUP_W // ML_HEADS
HY_CH = GROUP_W
HY_BANDS = 8
HY_HIDDEN = 64
HY_SIN_FREQ = 1.0
HY_FAST_DECAY = 0.3
HY_SLOW_DECAY = 1.5
HY_DECAY_TARGET = 1e-2
DN_HEADS = 4
DN_HEAD_DIM = GROUP_W // DN_HEADS
CHUNK = 64
N_EXPERTS = 32
TOP_K = 4
D_FF = D_MODEL
SWIGLU_LIMIT = 7.0
SWIGLU_ALPHA = 1.702
MOE_BLOCK = 256
ROPE_BASE = 10000.0
EPS = 1e-6

LANE = 128
HALF = D_MODEL // 2
NEG = -1e30
HI = lax.Precision.HIGHEST

NA_Q, NA_K, NA_V = 0, 4, 8
ML_Q, ML_K, ML_V, ML_O = 12, 16, 20, 24
HY_U = 28
DN_Q = 40
DN_G = 52
SMALL = 56
P_COLS = 60 * LANE


def _params(sem, vmem_mb):
    return pltpu.CompilerParams(dimension_semantics=sem, vmem_limit_bytes=vmem_mb << 20)


def _mean_sq(x):
    return jnp.mean(x * x, axis=-1, keepdims=True)


def _sigmoid(x):
    return 1.0 / (1.0 + jnp.exp(-x))


def _adaln_body(c_ref, w_ref, b_ref, o_ref):
    c = c_ref[...]
    s = (c * _sigmoid(c)).astype(BF16)
    o_ref[0] = jnp.dot(s, w_ref[0].astype(BF16), preferred_element_type=F32) + b_ref[0]


def _adaln(cc, w_ada, b_ada):
    D = cc.shape[1]
    tn = 1024
    return pl.pallas_call(
        _adaln_body,
        grid=(DEPTH, 6 * D // tn),
        in_specs=[pl.BlockSpec((16, D), lambda l, j: (0, 0)),
                  pl.BlockSpec((1, D, tn), lambda l, j: (l, 0, j)),
                  pl.BlockSpec((1, 1, tn), lambda l, j: (l, 0, j))],
        out_specs=pl.BlockSpec((1, 16, tn), lambda l, j: (l, 0, j)),
        out_shape=jax.ShapeDtypeStruct((DEPTH, 16, 6 * D), F32),
        compiler_params=_params(("arbitrary", "arbitrary"), 40),
        name="adaln",
    )(cc, w_ada, b_ada.reshape(DEPTH, 1, 6 * D))


def _inproj_body(x_ref, g_ref, sc_ref, sh_ref, w_ref, o_ref, h_scr):
    @pl.when(pl.program_id(1) == 0)
    def _():
        x = x_ref[...]
        y = x * lax.rsqrt(_mean_sq(x) + EPS) * g_ref[...]
        h_scr[...] = (y * (1.0 + sc_ref[0, 0]) + sh_ref[0, 0]).astype(BF16)

    o_ref[...] = jnp.dot(h_scr[...], w_ref[...], preferred_element_type=F32)


def _inproj(x2, seq_len, mod, g, w_p):
    T, D = x2.shape
    tm = min(1024, seq_len)
    tn = 512
    per_seq = seq_len // tm
    bmap = (lambda i: i // per_seq) if mod.shape[0] > 1 else (lambda i: 0)
    return pl.pallas_call(
        _inproj_body,
        grid=(T // tm, P_COLS // tn),
        in_specs=[pl.BlockSpec((tm, D), lambda i, j: (i, 0)),
                  pl.BlockSpec((1, D), lambda i, j: (0, 0)),
                  pl.BlockSpec((1, 1, 1, D), lambda i, j: (bmap(i), 1, 0, 0)),
                  pl.BlockSpec((1, 1, 1, D), lambda i, j: (bmap(i), 0, 0, 0)),
                  pl.BlockSpec((D, tn), lambda i, j: (0, j))],
        out_specs=pl.BlockSpec((tm, tn), lambda i, j: (i, j)),
        out_shape=jax.ShapeDtypeStruct((T, P_COLS), F32),
        scratch_shapes=[pltpu.VMEM((tm, D), BF16)],
        compiler_params=_params(("arbitrary", "arbitrary"), 48),
        name="inproj",
    )(x2, g, mod, mod, w_p)


def _pack_halves(t):
    hi = pltpu.bitcast(t[:, :HALF].astype(BF16).astype(F32), I32)
    lo = pltpu.bitcast(t[:, HALF:].astype(BF16).astype(F32), I32)
    return hi | lax.shift_right_logical(lo, 16)


def _unpack_halves(u):
    hi = pltpu.bitcast(u & jnp.int32(-65536), F32)
    lo = pltpu.bitcast(lax.shift_left(u, 16), F32)
    return hi, lo


def _outproj_body(ya_ref, yb_ref, yh_ref, yd_ref, w_ref, x_ref, g1n_ref, gate_ref, g2n_ref, sc_ref, sh_ref,
                  rw_ref, rb_ref, xo_ref, tok_ref, lg_ref):
    G = GROUP_W
    tm = x_ref.shape[0]
    for half in range(2):
        rows = slice(half * tm // 2, (half + 1) * tm // 2)
        y = jnp.dot(ya_ref[rows, :].astype(BF16), w_ref[0:G, :], preferred_element_type=F32)
        y += jnp.dot(yb_ref[rows, :].astype(BF16), w_ref[G:2 * G, :], preferred_element_type=F32)
        y += jnp.dot(yh_ref[rows, :].astype(BF16), w_ref[2 * G:3 * G, :], preferred_element_type=F32)
        y += jnp.dot(yd_ref[rows, :].astype(BF16), w_ref[3 * G:4 * G, :], preferred_element_type=F32)
        yn = y * lax.rsqrt(_mean_sq(y) + EPS) * g1n_ref[...]
        xn = x_ref[rows, :] + gate_ref[0, 0] * yn
        xo_ref[rows, :] = xn
        t = xn * lax.rsqrt(_mean_sq(xn) + EPS) * g2n_ref[...]
        t = t * (1.0 + sc_ref[0, 0]) + sh_ref[0, 0]
        tok_ref[rows, :] = _pack_halves(t)
        lg_ref[rows, :] = jnp.dot(t, rw_ref[...], preferred_element_type=F32, precision=HI) + rb_ref[...]


def _outproj(ys, w_out_bf, x2, seq_len, mod, g1n, g2n, router_w, router_b):
    T, D = x2.shape
    tm = 256
    per_seq = seq_len // tm
    bmap = (lambda i: i // per_seq) if mod.shape[0] > 1 else (lambda i: 0)
    row = lambda i: (i, 0)
    fix = lambda i: (0, 0)
    modspec = lambda k: pl.BlockSpec((1, 1, 1, D), lambda i: (bmap(i), k, 0, 0))
    return pl.pallas_call(
        _outproj_body,
        grid=(T // tm,),
        in_specs=[pl.BlockSpec((tm, GROUP_W), row)] * 4 + [
            pl.BlockSpec((D, D), fix), pl.BlockSpec((tm, D), row), pl.BlockSpec((1, D), fix), modspec(2),
            pl.BlockSpec((1, D), fix), modspec(4), modspec(3),
            pl.BlockSpec((D, N_EXPERTS), fix), pl.BlockSpec((1, N_EXPERTS), fix)],
        out_specs=[pl.BlockSpec((tm, D), row), pl.BlockSpec((tm, HALF), row), pl.BlockSpec((tm, N_EXPERTS), row)],
        out_shape=[jax.ShapeDtypeStruct((T, D), F32), jax.ShapeDtypeStruct((T, HALF), I32),
                   jax.ShapeDtypeStruct((T, N_EXPERTS), F32)],
        compiler_params=_params(("arbitrary",), 48),
        name="outproj",
    )(*ys, w_out_bf, x2, g1n, mod, g2n, mod, mod, router_w, router_b)


def _lane_select(cols, width, shape):
    lane = lax.broadcasted_iota(I32, shape, 1)
    out = jnp.broadcast_to(cols[width - 1], shape)
    for k in range(width - 2, -1, -1):
        out = jnp.where(lane == k, cols[k], out)
    return out


def _router_body(l_ref, idx_ref, rank_ref, gate_ref, cnt_ref, carry):
    @pl.when(pl.program_id(0) == 0)
    def _():
        carry[...] = jnp.zeros_like(carry)

    cur = l_ref[...]
    tm = cur.shape[0]
    lane = lax.broadcasted_iota(I32, cur.shape, 1)
    vals, ids, hots = [], [], []
    for _ in range(TOP_K):
        m = jnp.max(cur, axis=-1, keepdims=True)
        sel = jnp.min(jnp.where(cur == m, lane, N_EXPERTS), axis=-1, keepdims=True)
        hot = lane == sel
        vals.append(m)
        ids.append(sel)
        hots.append(hot)
        cur = jnp.where(hot, -jnp.inf, cur)
    es = [jnp.exp(v - vals[0]) for v in vals]
    tot = es[0] + es[1] + es[2] + es[3]
    cnt = sum(h.astype(F32) for h in hots)
    r = lax.broadcasted_iota(I32, (tm, tm), 0)
    c = lax.broadcasted_iota(I32, (tm, tm), 1)
    before = (c < r).astype(BF16)
    pre = jnp.dot(before, cnt.astype(BF16), preferred_element_type=F32) + carry[...]
    ranks = [jnp.sum(jnp.where(h, pre, 0.0), axis=-1, keepdims=True).astype(I32) for h in hots]
    carry[...] += jnp.sum(cnt, axis=0, keepdims=True)
    shape = (tm, TOP_K)
    idx_ref[...] = _lane_select(ids, TOP_K, shape)
    rank_ref[...] = _lane_select(ranks, TOP_K, shape)
    gate_ref[...] = _lane_select([e / tot for e in es], TOP_K, shape)
    cnt_ref[...] = carry[...].astype(I32)


def _router(logits):
    T = logits.shape[0]
    tm = 256
    row = lambda i: (i, 0)
    return pl.pallas_call(
        _router_body,
        grid=(T // tm,),
        in_specs=[pl.BlockSpec((tm, N_EXPERTS), row)],
        out_specs=[pl.BlockSpec((tm, TOP_K), row), pl.BlockSpec((tm, TOP_K), row), pl.BlockSpec((tm, TOP_K), row),
                   pl.BlockSpec((1, N_EXPERTS), lambda i: (0, 0))],
        out_shape=[jax.ShapeDtypeStruct((T, TOP_K), I32), jax.ShapeDtypeStruct((T, TOP_K), I32),
                   jax.ShapeDtypeStruct((T, TOP_K), F32), jax.ShapeDtypeStruct((1, N_EXPERTS), I32)],
        scratch_shapes=[pltpu.VMEM((1, N_EXPERTS), F32)],
        compiler_params=_params(("arbitrary",), 32),
        name="router",
    )(logits)


MOE_TM = 256


def _slot_copy(src_ref, dst_ref, src_row, dst_row, sem):
    return pltpu.make_async_copy(src_ref.at[pl.ds(src_row, 1)], dst_ref.at[pl.ds(dst_row, 1)], sem)


def _dispatch_body(dest_ref, tok_ref, xs_in_ref, xs_ref, sem):
    del xs_in_ref

    def start(r, carry):
        for k in range(TOP_K):
            _slot_copy(tok_ref, xs_ref, r, dest_ref[0, 0, r * TOP_K + k], sem).start()
        return carry

    lax.fori_loop(0, MOE_TM, start, 0, unroll=2)

    def wait(r, carry):
        for k in range(TOP_K):
            _slot_copy(tok_ref, xs_ref, 0, 0, sem).wait()
        return carry

    lax.fori_loop(0, MOE_TM, wait, 0, unroll=2)


def _dest_spec():
    return pl.BlockSpec((1, 1, MOE_TM * TOP_K), lambda i: (i, 0, 0), memory_space=pltpu.SMEM)


def _dispatch(dest3, tok, n_slots):
    T = tok.shape[0]
    tm = MOE_TM
    return pl.pallas_call(
        _dispatch_body,
        grid=(T // tm,),
        in_specs=[_dest_spec(), pl.BlockSpec((tm, HALF), lambda i: (i, 0)), pl.BlockSpec(memory_space=pl.ANY)],
        out_specs=pl.BlockSpec(memory_space=pl.ANY),
        scratch_shapes=[pltpu.SemaphoreType.DMA(())],
        out_shape=jax.ShapeDtypeStruct((n_slots, HALF), I32),
        input_output_aliases={2: 0},
        compiler_params=_params(("arbitrary",), 32),
        name="moe_dispatch",
    )(dest3, tok, jnp.zeros((n_slots, HALF), I32))


def _fresh_weights(be_ref, i):
    return jnp.logical_or(i == 0, be_ref[i] != be_ref[jnp.maximum(i - 1, 0)])


def _g1_body(be_ref, nu_ref, x_ref, wg_ref, wu_ref, bg_ref, bu_ref, h_ref, wg_bf, wu_bf):
    i = pl.program_id(1)
    used = i < nu_ref[0]

    @pl.when(jnp.logical_and(used, _fresh_weights(be_ref, i)))
    def _():
        wg_bf[...] = wg_ref[0, 0].astype(BF16)
        wu_bf[...] = wu_ref[0, 0].astype(BF16)

    @pl.when(used)
    def _():
        hi, lo = _unpack_halves(x_ref[...])
        hi = hi.astype(BF16)
        lo = lo.astype(BF16)

        def proj(w_bf, b_ref):
            return (jnp.dot(hi, w_bf[:HALF, :], preferred_element_type=F32)
                    + jnp.dot(lo, w_bf[HALF:, :], preferred_element_type=F32) + b_ref[0, 0])

        g = jnp.minimum(proj(wg_bf, bg_ref), SWIGLU_LIMIT)
        u = jnp.clip(proj(wu_bf, bu_ref), -SWIGLU_LIMIT, SWIGLU_LIMIT)
        h_ref[...] = (g * _sigmoid(SWIGLU_ALPHA * g) * (u + 1.0)).astype(BF16)

    @pl.when(jnp.logical_not(used))
    def _():
        h_ref[...] = jnp.zeros_like(h_ref)


def _expert_up(block_expert, n_used, xs, w_gu, b_gu4, layer):
    n_slots = xs.shape[0]
    nb = n_slots // MOE_BLOCK
    tn = 1024
    nj = D_FF // tn
    last = lambda i, nu: jnp.minimum(i, nu[0] - 1)
    wspec = lambda off: pl.BlockSpec((1, 1, D_MODEL, tn), lambda j, i, be, nu: (layer, be[last(i, nu)], 0, off + j))
    bspec = lambda off: pl.BlockSpec((1, 1, 1, tn), lambda j, i, be, nu: (layer, be[last(i, nu)], 0, off + j))
    return pl.pallas_call(
        _g1_body,
        grid_spec=pltpu.PrefetchScalarGridSpec(
            num_scalar_prefetch=2,
            grid=(nj, nb),
            in_specs=[pl.BlockSpec((MOE_BLOCK, HALF), lambda j, i, be, nu: (last(i, nu), 0)),
                      wspec(0), wspec(nj), bspec(0), bspec(nj)],
            out_specs=pl.BlockSpec((MOE_BLOCK, tn), lambda j, i, be, nu: (i, j)),
            scratch_shapes=[pltpu.VMEM((D_MODEL, tn), BF16), pltpu.VMEM((D_MODEL, tn), BF16)]),
        out_shape=jax.ShapeDtypeStruct((n_slots, D_FF), BF16),
        compiler_params=_params(("arbitrary", "arbitrary"), 56),
        name="moe_up",
    )(block_expert, n_used, xs, w_gu, w_gu, b_gu4, b_gu4)


def _g2_body(be_ref, nu_ref, h_ref, w_ref, b_ref, y_ref, w_bf):
    i = pl.program_id(0)
    used = i < nu_ref[0]

    @pl.when(jnp.logical_and(used, _fresh_weights(be_ref, i)))
    def _():
        w_bf[...] = w_ref[0, 0].astype(BF16)

    @pl.when(used)
    def _():
        y_ref[...] = _pack_halves(jnp.dot(h_ref[...], w_bf[...], preferred_element_type=F32) + b_ref[0, 0])

    @pl.when(jnp.logical_not(used))
    def _():
        y_ref[...] = jnp.zeros_like(y_ref)


def _expert_down(block_expert, n_used, h, w_down, b_down4, layer):
    n_slots = h.shape[0]
    nb = n_slots // MOE_BLOCK
    last = lambda i, nu: jnp.minimum(i, nu[0] - 1)
    return pl.pallas_call(
        _g2_body,
        grid_spec=pltpu.PrefetchScalarGridSpec(
            num_scalar_prefetch=2,
            grid=(nb,),
            in_specs=[pl.BlockSpec((MOE_BLOCK, D_FF), lambda i, be, nu: (last(i, nu), 0)),
                      pl.BlockSpec((1, 1, D_FF, D_MODEL), lambda i, be, nu: (layer, be[last(i, nu)], 0, 0)),
                      pl.BlockSpec((1, 1, 1, D_MODEL), lambda i, be, nu: (layer, be[last(i, nu)], 0, 0))],
            out_specs=pl.BlockSpec((MOE_BLOCK, HALF), lambda i, be, nu: (i, 0)),
            scratch_shapes=[pltpu.VMEM((D_FF, D_MODEL), BF16)]),
        out_shape=jax.ShapeDtypeStruct((n_slots, HALF), I32),
        compiler_params=_params(("arbitrary",), 56),
        name="moe_down",
    )(block_expert, n_used, h, w_down, b_down4)


def _combine_body(dest_ref, gate_ref, ys_ref, x_ref, gn_ref, g2_ref, o_ref, buf, sem):
    def start(r, carry):
        for k in range(TOP_K):
            _slot_copy(ys_ref, buf.at[k], dest_ref[0, 0, r * TOP_K + k], r, sem).start()
        return carry

    lax.fori_loop(0, MOE_TM, start, 0, unroll=2)

    def wait(r, carry):
        for k in range(TOP_K):
            _slot_copy(ys_ref, buf.at[k], 0, 0, sem).wait()
        return carry

    lax.fori_loop(0, MOE_TM, wait, 0, unroll=2)

    gate = gate_ref[...]
    f_hi = f_lo = None
    for k in range(TOP_K):
        hi, lo = _unpack_halves(buf[k])
        gk = gate[:, k:k + 1]
        f_hi = gk * hi if k == 0 else f_hi + gk * hi
        f_lo = gk * lo if k == 0 else f_lo + gk * lo
    ms = (jnp.sum(f_hi * f_hi, axis=-1, keepdims=True) + jnp.sum(f_lo * f_lo, axis=-1, keepdims=True)) / D_MODEL
    scale = lax.rsqrt(ms + EPS)
    gn = gn_ref[...]
    g2 = g2_ref[0, 0]
    o_ref[:, :HALF] = x_ref[:, :HALF] + g2[:, :HALF] * (f_hi * scale * gn[:, :HALF])
    o_ref[:, HALF:] = x_ref[:, HALF:] + g2[:, HALF:] * (f_lo * scale * gn[:, HALF:])


def _combine(dest3, gates, ys, x2, seq_len, mod, gn):
    T, D = x2.shape
    tm = MOE_TM
    per_seq = seq_len // tm
    bmap = (lambda i: i // per_seq) if mod.shape[0] > 1 else (lambda i: 0)
    return pl.pallas_call(
        _combine_body,
        grid=(T // tm,),
        in_specs=[_dest_spec(), pl.BlockSpec((tm, TOP_K), lambda i: (i, 0)),
                  pl.BlockSpec(memory_space=pl.ANY),
                  pl.BlockSpec((tm, D), lambda i: (i, 0)),
                  pl.BlockSpec((1, D), lambda i: (0, 0)),
                  pl.BlockSpec((1, 1, 1, D), lambda i: (bmap(i), 5, 0, 0))],
        out_specs=pl.BlockSpec((tm, D), lambda i: (i, 0)),
        scratch_shapes=[pltpu.VMEM((TOP_K, tm, HALF), I32), pltpu.SemaphoreType.DMA(())],
        out_shape=jax.ShapeDtypeStruct((T, D), F32),
        compiler_params=_params(("arbitrary",), 40),
        name="moe_combine",
    )(dest3, gates, ys, x2, gn, mod)


def _moe(tok, logits, w_gu, b_gu, w_down, b_down, layer):
    T = tok.shape[0]
    idx, rank, gates, counts = _router(logits)
    counts = counts[0]
    padded = (counts + MOE_BLOCK - 1) // MOE_BLOCK * MOE_BLOCK
    pad_end = jnp.cumsum(padded)
    pad_start = (pad_end - padded).astype(I32)
    n_slots = (T * TOP_K // MOE_BLOCK + N_EXPERTS) * MOE_BLOCK
    nb = n_slots // MOE_BLOCK
    blk = jnp.arange(nb, dtype=I32) * MOE_BLOCK
    block_expert = jnp.minimum(jnp.sum((blk[:, None] >= pad_end[None, :]).astype(I32), axis=1), N_EXPERTS - 1)
    n_used = (pad_end[-1:] // MOE_BLOCK).astype(I32)
    hot = idx[:, :, None] == jnp.arange(N_EXPERTS, dtype=I32)[None, None, :]
    dest = jnp.sum(jnp.where(hot, pad_start[None, None, :], 0), axis=-1) + rank
    dest3 = dest.reshape(T // MOE_TM, 1, MOE_TM * TOP_K)
    xs = _dispatch(dest3, tok, n_slots)
    h = _expert_up(block_expert, n_used, xs, w_gu, b_gu.reshape(DEPTH, N_EXPERTS, 1, 2 * D_FF), layer)
    ys = _expert_down(block_expert, n_used, h, w_down, b_down.reshape(DEPTH, N_EXPERTS, 1, D_MODEL), layer)
    return dest3, gates, ys


NA_QT = 4
NA_BAND = 12
NA_ROWS = 32


def _na_bias_table(rpb):
    W = GRID_W
    cols = np.arange(W)
    cs = np.clip(cols - NA_WIN_W // 2, 0, W - NA_WIN_W)
    valid_c = (cols[None, :] >= cs[:, None]) & (cols[None, :] < cs[:, None] + NA_WIN_W)
    dc = np.clip(cols[None, :] - cols[:, None] + NA_WIN_W - 1, 0, 2 * NA_WIN_W - 2)
    tm = jnp.where(valid_c, rpb[:, :, dc], NEG)
    nt = NA_ROWS // NA_QT
    qt = np.arange(nt)
    r = NA_QT * qt[:, None] + np.arange(NA_QT)[None, :]
    kb = np.clip(NA_QT * qt - NA_WIN_H // 2, 0, NA_ROWS - NA_BAND)
    kr = kb[:, None] + np.arange(NA_BAND)[None, :]
    rs = np.clip(r - NA_WIN_H // 2, 0, NA_ROWS - NA_WIN_H)
    valid_r = (kr[:, None, :] >= rs[:, :, None]) & (kr[:, None, :] < rs[:, :, None] + NA_WIN_H)
    dr = np.clip(kr[:, None, :] - r[:, :, None] + NA_WIN_H - 1, 0, 2 * NA_WIN_H - 2)
    blocks = jnp.where(valid_r[None, :, :, :, None, None], tm[:, dr], NEG)
    return blocks.transpose(1, 0, 2, 4, 3, 5).reshape(nt, NA_HEADS, NA_QT * W, NA_BAND * W)


def _masked_heads_attention(q, keys, values, biases):
    lane = lax.broadcasted_iota(I32, q.shape, 1)
    nt = (((1,), (1,)), ((), ()))
    out = None
    for hh in range(2):
        hm = (lane >= NA_HEAD_DIM) == bool(hh)
        qh = jnp.where(hm, q, 0.0).astype(BF16)
        ss = []
        for kk, bb in zip(keys, biases):
            s = lax.dot_general(qh, kk, nt, preferred_element_type=F32)
            ss.append(s if bb is None else s + bb[hh])
        m = ss[0].max(axis=1, keepdims=True)
        for s in ss[1:]:
            m = jnp.maximum(m, s.max(axis=1, keepdims=True))
        den = None
        o = None
        for s, vv in zip(ss, values):
            p = jnp.exp(s - m)
            d = jnp.sum(p, axis=1, keepdims=True)
            den = d if den is None else den + d
            pv = jnp.dot(p.astype(BF16), vv, preferred_element_type=F32)
            o = pv if o is None else o + pv
        o = o / den
        out = o if hh == 0 else jnp.where(hm, o, out)
    return out


def _na_body(q_ref, k_ref, v_ref, kc_ref, vc_ref, b_ref, o_ref):
    qt = pl.program_id(2)
    kb = pl.multiple_of(jnp.clip(qt * NA_QT - NA_WIN_H // 2, 0, NA_ROWS - NA_BAND) * GRID_W, GRID_W)
    band = NA_BAND * GRID_W
    q = q_ref[0] * (NA_HEAD_DIM ** -0.5)
    kl = k_ref[0, pl.ds(kb, band), :].astype(BF16)
    vl = v_ref[0, pl.ds(kb, band), :].astype(BF16)
    o_ref[0] = _masked_heads_attention(q, [kl, kc_ref[0].astype(BF16)], [vl, vc_ref[0].astype(BF16)],
                                       [(b_ref[0, 0], b_ref[0, 1]), None])


def _na_attention(p, pc, bias):
    B, L, _ = p.shape
    Lc = pc.shape[1]
    assert L == NA_ROWS * GRID_W
    tq = NA_QT * GRID_W
    return pl.pallas_call(
        _na_body,
        grid=(B, NA_HEADS // 2, L // tq),
        in_specs=[pl.BlockSpec((1, tq, LANE), lambda b, h, t: (b, t, NA_Q + h)),
                  pl.BlockSpec((1, L, LANE), lambda b, h, t: (b, 0, NA_K + h)),
                  pl.BlockSpec((1, L, LANE), lambda b, h, t: (b, 0, NA_V + h)),
                  pl.BlockSpec((1, Lc, LANE), lambda b, h, t: (b, 0, NA_K + h)),
                  pl.BlockSpec((1, Lc, LANE), lambda b, h, t: (b, 0, NA_V + h)),
                  pl.BlockSpec((1, 2, tq, NA_BAND * GRID_W), lambda b, h, t: (t, h, 0, 0))],
        out_specs=pl.BlockSpec((1, tq, LANE), lambda b, h, t: (b, t, h)),
        out_shape=jax.ShapeDtypeStruct((B, L, GROUP_W), F32),
        compiler_params=_params(("arbitrary", "arbitrary", "arbitrary"), 40),
        name="na_attention",
    )(p, p, p, pc, pc, bias)


def _ctx_attn_body(q_ref, k_ref, v_ref, o_ref):
    q = q_ref[0] * (NA_HEAD_DIM ** -0.5)
    o_ref[0] = _masked_heads_attention(q, [k_ref[0].astype(BF16)], [v_ref[0].astype(BF16)], [None])


def _ctx_attention(pc):
    B, Lc, _ = pc.shape
    return pl.pallas_call(
        _ctx_attn_body,
        grid=(B, NA_HEADS // 2),
        in_specs=[pl.BlockSpec((1, Lc, LANE), lambda b, h: (b, 0, NA_Q + h)),
                  pl.BlockSpec((1, Lc, LANE), lambda b, h: (b, 0, NA_K + h)),
                  pl.BlockSpec((1, Lc, LANE), lambda b, h: (b, 0, NA_V + h))],
        out_specs=pl.BlockSpec((1, Lc, LANE), lambda b, h: (b, 0, h)),
        out_shape=jax.ShapeDtypeStruct((B, Lc, GROUP_W), F32),
        compiler_params=_params(("arbitrary", "arbitrary"), 32),
        name="ctx_attention",
    )(pc, pc, pc)


CONV_TT = 256
HALO = 8


def _conv3_body(xp_ref, x_ref, xn_ref, w_ref, b_ref, o_ref, *, silu, nt):
    i = pl.program_id(1)
    x = x_ref[0]
    tt = x.shape[0]
    prev_row = jnp.where(i == 0, 0.0, xp_ref[0, HALO - 1:HALO, :])
    next_row = jnp.where(i == nt - 1, 0.0, xn_ref[0, 0:1, :])
    row = lax.broadcasted_iota(I32, x.shape, 0)
    xm = jnp.where(row == 0, prev_row, pltpu.roll(x, 1, 0))
    xp = jnp.where(row == tt - 1, next_row, pltpu.roll(x, tt - 1, 0))
    y = xm * w_ref[0:1, :] + x * w_ref[1:2, :] + xp * w_ref[2:3, :] + b_ref[...]
    o_ref[0] = y * _sigmoid(y) if silu else y


def _conv3(p, col0, w, b, silu):
    B, L, _ = p.shape
    C = w.shape[1]
    cw = 512
    tt = min(CONV_TT, L)
    nt = L // tt
    hb = tt // HALO
    cb = col0 * LANE // cw
    assert col0 * LANE % cw == 0
    return pl.pallas_call(
        functools.partial(_conv3_body, silu=silu, nt=nt),
        grid=(B, nt, C // cw),
        in_specs=[pl.BlockSpec((1, HALO, cw), lambda bb, i, j: (bb, jnp.maximum(i * hb - 1, 0), cb + j)),
                  pl.BlockSpec((1, tt, cw), lambda bb, i, j: (bb, i, cb + j)),
                  pl.BlockSpec((1, HALO, cw), lambda bb, i, j: (bb, jnp.minimum((i + 1) * hb, L // HALO - 1), cb + j)),
                  pl.BlockSpec((3, cw), lambda bb, i, j: (0, j)),
                  pl.BlockSpec((1, cw), lambda bb, i, j: (0, j))],
        out_specs=pl.BlockSpec((1, tt, cw), lambda bb, i, j: (bb, i, j)),
        out_shape=jax.ShapeDtypeStruct((B, L, C), F32),
        compiler_params=_params(("arbitrary", "arbitrary", "arbitrary"), 32),
        name="conv3",
    )(p, p, p, w, b)


def _dft_tables(L):
    N = 2 * L
    f = jnp.arange(L, dtype=I32)[:, None]
    t = jnp.arange(L, dtype=I32)[None, :]
    ang = ((f * t) % N).astype(F32) * (2.0 * math.pi / N)
    c, s = jnp.cos(ang), jnp.sin(ang)
    nyq = jnp.where(t % 2 == 0, 1.0, -1.0).astype(F32)
    dc = f == 0
    fc = c
    fs = jnp.where(dc, nyq, -s)
    ci = (jnp.where(dc, 1.0, 2.0) * c / N).T
    si = (jnp.where(dc, nyq, -2.0 * s) / N).T
    return fc.astype(BF16), fs.astype(BF16), ci.astype(BF16), si.astype(BF16)


def _hy_filter_body(w1_ref, b1_ref, w2_ref, b2_ref, w3_ref, b3_ref, win_ref, o_ref, *, L, tt):
    i = pl.program_id(0)
    pos = (lax.broadcasted_iota(I32, (tt, LANE), 0) + i * tt).astype(F32) / L
    lane = lax.broadcasted_iota(I32, (tt, LANE), 1)
    band = jnp.where(lane <= HY_BANDS, lane, lane - HY_BANDS).astype(F32)
    ang = (2.0 * math.pi) * pos * band
    feats = jnp.where(lane == 0, pos, jnp.where(lane <= HY_BANDS, jnp.sin(ang),
                                                jnp.where(lane <= 2 * HY_BANDS, jnp.cos(ang), 0.0)))
    h = jnp.sin(HY_SIN_FREQ * (jnp.dot(feats, w1_ref[...], preferred_element_type=F32, precision=HI) + b1_ref[...]))
    h = jnp.sin(HY_SIN_FREQ * (jnp.dot(h, w2_ref[...], preferred_element_type=F32, precision=HI) + b2_ref[...]))
    h = jnp.dot(h, w3_ref[...], preferred_element_type=F32, precision=HI) + b3_ref[...]
    t1 = (lax.broadcasted_iota(I32, (tt, 1), 0) + i * tt).astype(F32) / L
    h = h * jnp.exp(-t1 * win_ref[...])
    row = lax.broadcasted_iota(I32, h.shape, 0) + i * tt
    col = lax.broadcasted_iota(I32, h.shape, 1)
    bwd = (col // HY_CH) % 2 == 1
    o_ref[...] = jnp.where(jnp.logical_and(row == 0, bwd), 0.0, h)


def _hy_filters(L, w1, b1, w2, b2, w3, b3):
    tt = min(256, L)
    nc = 4 * HY_CH
    w1p = jnp.zeros((LANE, HY_HIDDEN), F32).at[:w1.shape[0]].set(w1)
    max_decay = math.log(HY_DECAY_TARGET) / HY_FAST_DECAY
    min_decay = math.log(HY_DECAY_TARGET) / HY_SLOW_DECAY
    deltas = jnp.abs(jnp.linspace(min_decay, max_decay, HY_CH, dtype=F32))
    win = jnp.tile(deltas, 4)[None, :]
    fix = lambda i: (0, 0)
    return pl.pallas_call(
        functools.partial(_hy_filter_body, L=L, tt=tt),
        grid=(L // tt,),
        in_specs=[pl.BlockSpec((LANE, HY_HIDDEN), fix), pl.BlockSpec((1, HY_HIDDEN), fix),
                  pl.BlockSpec((HY_HIDDEN, HY_HIDDEN), fix), pl.BlockSpec((1, HY_HIDDEN), fix),
                  pl.BlockSpec((HY_HIDDEN, nc), fix), pl.BlockSpec((1, nc), fix), pl.BlockSpec((1, nc), fix)],
        out_specs=pl.BlockSpec((tt, nc), lambda i: (i, 0)),
        out_shape=jax.ShapeDtypeStruct((L, nc), F32),
        compiler_params=_params(("arbitrary",), 32),
        name="hyena_filters",
    )(w1p, b1[None, :], w2, b2[None, :], w3, b3[None, :], win)


def _hy_spectrum_body(fc_ref, fs_ref, h_ref, gc_ref, gs_ref):
    hb = h_ref[...].astype(BF16)
    gc_ref[...] = jnp.dot(fc_ref[...], hb, preferred_element_type=F32)
    gs_ref[...] = jnp.dot(fs_ref[...], hb, preferred_element_type=F32)


def _hy_spectrum(fc, fs, hmat):
    L, nc = hmat.shape
    tf = min(512, L)
    tn = 512
    return pl.pallas_call(
        _hy_spectrum_body,
        grid=(nc // tn, L // tf),
        in_specs=[pl.BlockSpec((tf, L), lambda j, i: (i, 0)), pl.BlockSpec((tf, L), lambda j, i: (i, 0)),
                  pl.BlockSpec((L, tn), lambda j, i: (0, j))],
        out_specs=[pl.BlockSpec((tf, tn), lambda j, i: (i, j)), pl.BlockSpec((tf, tn), lambda j, i: (i, j))],
        out_shape=[jax.ShapeDtypeStruct((L, nc), F32), jax.ShapeDtypeStruct((L, nc), F32)],
        compiler_params=_params(("arbitrary", "arbitrary"), 40),
        name="hyena_spectrum",
    )(fc, fs, hmat)


def _hy_fwd_body(fc_ref, fs_ref, z_ref, gc_ref, gs_ref, yre_ref, yim_ref, zb):
    i = pl.program_id(1)

    @pl.when(i == 0)
    def _():
        zb[...] = z_ref[0].astype(BF16)

    C = HY_CH
    zre = jnp.dot(fc_ref[...], zb[...], preferred_element_type=F32)
    zim = jnp.dot(fs_ref[...], zb[...], preferred_element_type=F32)
    gre = gc_ref[:, :C] + gc_ref[:, C:]
    a = gs_ref[:, :C]
    b = gs_ref[:, C:]
    packed = jnp.logical_and(lax.broadcasted_iota(I32, zre.shape, 0) == 0, i == 0)
    gim = jnp.where(packed, a + b, a - b)
    yre_ref[0] = jnp.where(packed, zre * gre, zre * gre - zim * gim).astype(BF16)
    yim_ref[0] = jnp.where(packed, zim * gim, zre * gim + zim * gre).astype(BF16)


def _hy_forward(fc, fs, z_src, z_col, gc, gs, order):
    B, L, _ = z_src.shape
    C = HY_CH
    tf = min(512, L)
    return pl.pallas_call(
        _hy_fwd_body,
        grid=(B, L // tf),
        in_specs=[pl.BlockSpec((tf, L), lambda b, i: (i, 0)), pl.BlockSpec((tf, L), lambda b, i: (i, 0)),
                  pl.BlockSpec((1, L, C), lambda b, i: (b, 0, z_col)),
                  pl.BlockSpec((tf, 2 * C), lambda b, i: (i, order)), pl.BlockSpec((tf, 2 * C), lambda b, i: (i, order))],
        out_specs=[pl.BlockSpec((1, tf, C), lambda b, i: (b, i, 0)), pl.BlockSpec((1, tf, C), lambda b, i: (b, i, 0))],
        out_shape=[jax.ShapeDtypeStruct((B, L, C), BF16), jax.ShapeDtypeStruct((B, L, C), BF16)],
        scratch_shapes=[pltpu.VMEM((L, C), BF16)],
        compiler_params=_params(("arbitrary", "arbitrary"), 40),
        name="hyena_forward",
    )(fc, fs, z_src, gc, gs)


def _hy_inv_body(ci_ref, si_ref, yre_ref, yim_ref, z_ref, x_ref, skip_ref, o_ref):
    y = jnp.dot(ci_ref[...], yre_ref[0], preferred_element_type=F32)
    y += jnp.dot(si_ref[...], yim_ref[0], preferred_element_type=F32)
    o_ref[0] = x_ref[0] * (y + z_ref[0] * skip_ref[0])


def _hy_inverse(ci, si, yre, yim, z_src, z_col, x_src, x_col, skip, order):
    B, L, C = yre.shape
    tt = min(512, L)
    return pl.pallas_call(
        _hy_inv_body,
        grid=(B, L // tt),
        in_specs=[pl.BlockSpec((tt, L), lambda b, i: (i, 0)), pl.BlockSpec((tt, L), lambda b, i: (i, 0)),
                  pl.BlockSpec((1, L, C), lambda b, i: (b, 0, 0)), pl.BlockSpec((1, L, C), lambda b, i: (b, 0, 0)),
                  pl.BlockSpec((1, tt, C), lambda b, i: (b, i, z_col)),
                  pl.BlockSpec((1, tt, C), lambda b, i: (b, i, x_col)),
                  pl.BlockSpec((1, 1, C), lambda b, i: (order, 0, 0))],
        out_specs=pl.BlockSpec((1, tt, C), lambda b, i: (b, i, 0)),
        out_shape=jax.ShapeDtypeStruct((B, L, C), F32),
        compiler_params=_params(("arbitrary", "arbitrary"), 40),
        name="hyena_inverse",
    )(ci, si, yre, yim, z_src, x_src, skip)


def _hyena(p, tables, conv_w, conv_b, w1, b1, w2, b2, w3, b3, skip):
    L = p.shape[1]
    u = _conv3(p, HY_U, conv_w, conv_b[None, :], silu=False)
    fc, fs, ci, si = tables
    gc, gs = _hy_spectrum(fc, fs, _hy_filters(L, w1, b1, w2, b2, w3, b3))
    skip3 = skip[:, None, :]
    yre, yim = _hy_forward(fc, fs, u, 2, gc, gs, 0)
    z1 = _hy_inverse(ci, si, yre, yim, u, 2, u, 0, skip3, 0)
    yre, yim = _hy_forward(fc, fs, z1, 0, gc, gs, 1)
    return _hy_inverse(ci, si, yre, yim, z1, 0, u, 1, skip3, 1)


REC_HP = 4
REC_UNROLL = 2
NT_DIMS = (((1,), (1,)), ((), ()))
TN_DIMS = (((0,), (0,)), ((), ()))


def _chunk_masks(reverse):
    r = lax.broadcasted_iota(I32, (CHUNK, CHUNK), 0)
    c = lax.broadcasted_iota(I32, (CHUNK, CHUNK), 1)
    if reverse:
        return (r >= c).astype(F32), c >= r, c > r, r == c
    return (r <= c).astype(F32), c <= r, c < r, r == c


def _to_col(row, eye):
    return jnp.sum(jnp.where(eye, row, 0.0), axis=1, keepdims=True)


def _log_sigmoid(x):
    return jnp.minimum(x, 0.0) - jnp.log1p(jnp.exp(-jnp.abs(x)))


def _softplus(x):
    return jnp.maximum(x, 0.0) + jnp.log1p(jnp.exp(-jnp.abs(x)))


def _bdot(a, b):
    return jnp.dot(a.astype(BF16), b.astype(BF16), preferred_element_type=F32)


def _split_bf16(a):
    hi = a.astype(BF16)
    return hi, (a - hi.astype(F32)).astype(BF16)


def _dot3(a, b):
    d = lambda x, y: jnp.dot(x, y, preferred_element_type=F32)
    return d(a[0], b[0]) + (d(a[0], b[1]) + d(a[1], b[0]))


def _bdot_g(a, b, dims):
    return lax.dot_general(a.astype(BF16), b.astype(BF16), dims, preferred_element_type=F32)


def _ml_steps(items, c_refs, n_refs, m_refs):
    n = range(len(items))
    q, k, v, i_row, f_row, rev = zip(*items)
    masks = [_chunk_masks(r) for r in rev]
    incl, mask, _, eye = zip(*masks)
    b_row = [jnp.dot(f_row[i], incl[i], preferred_element_type=F32, precision=HI) for i in n]
    qk = [_bdot_g(q[i], k[i], NT_DIMS) for i in n]
    cst = [c_refs[i][...] for i in n]
    nst = [n_refs[i][...] for i in n]
    m = [m_refs[i][...] for i in n]
    qc = [_bdot(q[i], cst[i]) for i in n]
    b_col = [_to_col(b_row[i], eye[i]) for i in n]
    gi_row = [i_row[i] - b_row[i] for i in n]
    dlog = [jnp.where(mask[i], b_col[i] + gi_row[i], -jnp.inf) for i in n]
    a_col = [b_col[i] + m[i] for i in n]
    mt = [jnp.maximum(a_col[i], jnp.max(dlog[i], axis=1, keepdims=True)) for i in n]
    s = [qk[i] * jnp.exp(dlog[i] - mt[i]) for i in n]
    w_inter = [jnp.exp(a_col[i] - mt[i]) for i in n]
    sv = [_bdot(s[i], v[i]) for i in n]
    b_last = [b_row[i][:, 0:1] if rev[i] else b_row[i][:, CHUNK - 1:CHUNK] for i in n]
    g_row = [b_last[i] + gi_row[i] for i in n]
    m_new = [jnp.maximum(b_last[i] + m[i], jnp.max(g_row[i], axis=1, keepdims=True)) for i in n]
    kw = [k[i] * _to_col(jnp.exp(g_row[i] - m_new[i]), eye[i]) for i in n]
    kv = [_bdot_g(kw[i], v[i], TN_DIMS) for i in n]
    decay = [jnp.exp(b_last[i] + m[i] - m_new[i]) for i in n]
    hs = []
    for i in n:
        num = sv[i] + w_inter[i] * qc[i]
        den = jnp.sum(s[i], axis=1, keepdims=True) + w_inter[i] * jnp.sum(q[i] * nst[i], axis=1, keepdims=True)
        hs.append(num / jnp.maximum(jnp.abs(den), jnp.exp(-mt[i])))
        c_refs[i][...] = decay[i] * cst[i] + kv[i]
        n_refs[i][...] = decay[i] * nst[i] + jnp.sum(kw[i], axis=0, keepdims=True)
        m_refs[i][...] = m_new[i]
    return hs


def _mlstm_body(ql_ref, kl_ref, vl_ref, ol_ref, qc_ref, kc_ref, vc_ref, oc_ref, gl_ref, gc_ref, gb_ref,
                cos_ref, sin_ref, ng_ref, *rest, need_ctx):
    nchain = 2 * REC_HP
    n_out = 2 if need_ctx else 1
    y_ref = rest[0]
    yc_ref = rest[1] if need_ctx else None
    c_scr = rest[n_out:n_out + nchain]
    n_scr = rest[n_out + nchain:n_out + 2 * nchain]
    m_scr = rest[n_out + 2 * nchain:n_out + 3 * nchain]
    hl = rest[n_out + 3 * nchain]
    hc = rest[n_out + 3 * nchain + 1] if need_ctx else None
    for ref in (*c_scr, *n_scr, *m_scr):
        ref[...] = jnp.zeros_like(ref)

    def run(q_ref, k_ref, v_ref, g_ref, h_scr, rope):
        nc = g_ref.shape[3]

        def body(j, carry):
            items, where = [], []
            for hh in range(REC_HP):
                lanes = slice(hh * LANE, (hh + 1) * LANE)
                for d in range(2):
                    jj = nc - 1 - j if d else j
                    off = pl.multiple_of(jj * CHUNK, CHUNK)
                    q = q_ref[0, pl.ds(off, CHUNK), lanes]
                    k = k_ref[0, pl.ds(off, CHUNK), lanes]
                    v = v_ref[0, pl.ds(off, CHUNK), lanes]
                    if rope:
                        cs = cos_ref[pl.ds(off, CHUNK), :]
                        sn = sin_ref[pl.ds(off, CHUNK), :]
                        q = q * cs + pltpu.roll(q, ML_HEAD_DIM // 2, 1) * sn
                        k = k * cs + pltpu.roll(k, ML_HEAD_DIM // 2, 1) * sn
                    q = q * (ML_HEAD_DIM ** -0.5)
                    i_row = g_ref[0, hh, 2 * d, pl.ds(jj, 1), :] + gb_ref[hh, 2 * d]
                    f_row = _log_sigmoid(g_ref[0, hh, 2 * d + 1, pl.ds(jj, 1), :] + gb_ref[hh, 2 * d + 1])
                    items.append((q, k, v, i_row, f_row, bool(d)))
                    where.append((d, off, lanes))
            hs = _ml_steps(items, c_scr, n_scr, m_scr)
            if h_scr is not None:
                for h, (d, off, lanes) in zip(hs, where):
                    h_scr[d, pl.ds(off, CHUNK), lanes] = h
            return carry

        lax.fori_loop(0, nc, body, 0, unroll=REC_UNROLL)

    def finish(h_scr, o_ref, out_ref):
        for hh in range(REC_HP):
            lanes = slice(hh * LANE, (hh + 1) * LANE)
            h = h_scr[0, :, lanes] + h_scr[1, :, lanes]
            dev = h - jnp.mean(h, axis=-1, keepdims=True)
            var = jnp.mean(dev * dev, axis=-1, keepdims=True)
            out_ref[0, :, lanes] = dev * lax.rsqrt(var + EPS) * ng_ref[:, lanes] * _sigmoid(o_ref[0, :, lanes])

    run(qc_ref, kc_ref, vc_ref, gc_ref, hc, False)
    run(ql_ref, kl_ref, vl_ref, gl_ref, hl, True)
    finish(hl, ol_ref, y_ref)
    if need_ctx:
        finish(hc, oc_ref, yc_ref)


def _rope_tables(L):
    n_freq = ML_HEAD_DIM // 4
    t = jnp.arange(L)
    row = (t // GRID_W).astype(F32)
    col = (t % GRID_W).astype(F32)
    inv = ROPE_BASE ** (-jnp.arange(n_freq, dtype=F32) / n_freq)
    ang = jnp.concatenate([row[:, None] * inv, col[:, None] * inv], axis=-1)
    c, s = jnp.cos(ang), jnp.sin(ang)
    return jnp.concatenate([c, c], axis=-1), jnp.concatenate([-s, s], axis=-1)


def _gate_rows(g, n_types, heads):
    B, L, _ = g.shape
    return g.reshape(B, L, n_types, heads).transpose(0, 3, 2, 1).reshape(B, heads, n_types, L // CHUNK, CHUNK)


def _mlstm(p, pc, rope, gate_b, norm_g, need_ctx):
    B, L, _ = p.shape
    Lc = pc.shape[1]
    H = ML_HEADS
    gl = _gate_rows(p[:, :, SMALL * LANE:SMALL * LANE + 4 * H], 4, H)
    gc = _gate_rows(pc[:, :, SMALL * LANE:SMALL * LANE + 4 * H], 4, H)
    gb = gate_b.T.reshape(H, 4, 1, 1)
    cos, sin = rope
    W = REC_HP * LANE
    nchain = 2 * REC_HP
    lat = lambda col: pl.BlockSpec((1, L, W), lambda b, h: (b, 0, col // REC_HP + h), pipeline_mode=pl.Buffered(1))
    ctx = lambda col: pl.BlockSpec((1, Lc, W), lambda b, h: (b, 0, col // REC_HP + h))
    gspec = lambda n: pl.BlockSpec((1, REC_HP, 4, n // CHUNK, CHUNK), lambda b, h: (b, h, 0, 0, 0))
    fix = lambda b, h: (0, 0)
    out_specs = [pl.BlockSpec((1, L, W), lambda b, h: (b, 0, h))]
    out_shape = [jax.ShapeDtypeStruct((B, L, GROUP_W), F32)]
    scratch = ([pltpu.VMEM((ML_HEAD_DIM, ML_HEAD_DIM), F32)] * nchain + [pltpu.VMEM((1, ML_HEAD_DIM), F32)] * nchain
               + [pltpu.VMEM((1, 1), F32)] * nchain + [pltpu.VMEM((2, L, W), F32)])
    if need_ctx:
        out_specs.append(pl.BlockSpec((1, Lc, W), lambda b, h: (b, 0, h)))
        out_shape.append(jax.ShapeDtypeStruct((B, Lc, GROUP_W), F32))
        scratch.append(pltpu.VMEM((2, Lc, W), F32))
    res = pl.pallas_call(
        functools.partial(_mlstm_body, need_ctx=need_ctx),
        grid=(B, H // REC_HP),
        in_specs=[lat(ML_Q), lat(ML_K), lat(ML_V), lat(ML_O), ctx(ML_Q), ctx(ML_K), ctx(ML_V), ctx(ML_O),
                  gspec(L), gspec(Lc), pl.BlockSpec((REC_HP, 4, 1, 1), lambda b, h: (h, 0, 0, 0)),
                  pl.BlockSpec((L, LANE), fix), pl.BlockSpec((L, LANE), fix),
                  pl.BlockSpec((1, W), lambda b, h: (0, h))],
        out_specs=out_specs,
        out_shape=out_shape,
        scratch_shapes=scratch,
        compiler_params=_params(("arbitrary", "arbitrary"), 52),
        name="mlstm",
    )(p, p, p, p, pc, pc, pc, pc, gl, gc, gb, cos, sin, norm_g[None, :])
    return (res[0], res[1]) if need_ctx else (res[0], None)


def _dn_prep(items):
    n = range(len(items))
    q, k, v, a_row, beta_row, rev = zip(*items)
    masks = [_chunk_masks(r) for r in rev]
    incl, mask, strict, eye = zip(*masks)
    g_row = [jnp.dot(a_row[i], incl[i], preferred_element_type=F32, precision=HI) for i in n]
    g_col = [_to_col(g_row[i], eye[i]) for i in n]
    beta_col = [_to_col(beta_row[i], eye[i]) for i in n]
    dec = [jnp.exp(jnp.where(mask[i], g_col[i] - g_row[i], -jnp.inf)) for i in n]
    kbeta = [k[i] * beta_col[i] for i in n]
    eg_col = [jnp.exp(g_col[i]) for i in n]
    kk = [_bdot_g(kbeta[i], k[i], NT_DIMS) for i in n]
    x = [-jnp.where(strict[i], kk[i] * dec[i], 0.0) for i in n]
    inv = [eye[i].astype(F32) + x[i] for i in n]
    xs = [_split_bf16(x[i]) for i in n]
    for _ in range(5):
        x = [_dot3(xs[i], xs[i]) for i in n]
        xs = [_split_bf16(x[i]) for i in n]
        invs = [_split_bf16(inv[i]) for i in n]
        inv = [inv[i] + _dot3(invs[i], xs[i]) for i in n]
    invs = [_split_bf16(inv[i]) for i in n]
    u = [_dot3(invs[i], _split_bf16(v[i] * beta_col[i])) for i in n]
    w = [_dot3(invs[i], _split_bf16(kbeta[i] * eg_col[i])) for i in n]
    attn = [_bdot_g(q[i], k[i], NT_DIMS) * dec[i] for i in n]
    g_last = [g_row[i][:, 0:1] if rev[i] else g_row[i][:, CHUNK - 1:CHUNK] for i in n]
    qe = [q[i] * eg_col[i] for i in n]
    kd = [k[i] * jnp.exp(g_last[i] - g_col[i]) for i in n]
    return [(u[i], w[i], attn[i], qe[i], kd[i], jnp.exp(g_last[i])) for i in n]


def _dn_apply(preps, s_refs):
    n = range(len(preps))
    u, w, attn, qe, kd, eg_last = zip(*preps)
    st = [s_refs[i][...] for i in n]
    ws = [_bdot(w[i], st[i]) for i in n]
    qs = [_bdot(qe[i], st[i]) for i in n]
    v_new = [u[i] - ws[i] for i in n]
    av = [_bdot(attn[i], v_new[i]) for i in n]
    kv = [_bdot_g(kd[i], v_new[i], TN_DIMS) for i in n]
    for i in n:
        s_refs[i][...] = eg_last[i] * st[i] + kv[i]
    return [qs[i] + av[i] for i in n]


def _deltanet_body(ql_ref, kl_ref, vl_ref, zl_ref, qc_ref, kc_ref, vc_ref, zc_ref, gl_ref, gc_ref, prm_ref,
                   ng_ref, *rest, need_ctx):
    nchain = 2 * REC_HP
    n_out = 2 if need_ctx else 1
    y_ref = rest[0]
    yc_ref = rest[1] if need_ctx else None
    s_scr = rest[n_out:n_out + nchain]
    ol = rest[n_out + nchain]
    oc = rest[n_out + nchain + 1] if need_ctx else None
    for ref in s_scr:
        ref[...] = jnp.zeros_like(ref)

    def unit(x):
        return x * lax.rsqrt(jnp.sum(x * x, axis=-1, keepdims=True) + EPS)

    def run(q_ref, k_ref, v_ref, g_ref, o_scr):
        nc = g_ref.shape[3]

        chains = [(hh, d) for hh in range(REC_HP) for d in range(2)]

        def body(j2, carry):
            items, where = [], []
            for t in range(REC_UNROLL):
                j = j2 * REC_UNROLL + t
                for hh, d in chains:
                    lanes = slice(hh * LANE, (hh + 1) * LANE)
                    jj = nc - 1 - j if d else j
                    off = pl.multiple_of(jj * CHUNK, CHUNK)
                    q = unit(q_ref[0, pl.ds(off, CHUNK), lanes]) * (DN_HEAD_DIM ** -0.5)
                    k = unit(k_ref[0, pl.ds(off, CHUNK), lanes])
                    v = v_ref[0, pl.ds(off, CHUNK), lanes]
                    a_raw = g_ref[0, hh, 2 * d, pl.ds(jj, 1), :]
                    a_row = -jnp.exp(prm_ref[hh, 2 * d]) * _softplus(a_raw + prm_ref[hh, 2 * d + 1])
                    beta_row = _sigmoid(g_ref[0, hh, 2 * d + 1, pl.ds(jj, 1), :])
                    items.append((q, k, v, a_row, beta_row, bool(d)))
                    where.append((d, off, lanes))
            preps = _dn_prep(items)
            nch = len(chains)
            for t in range(REC_UNROLL):
                outs = _dn_apply(preps[t * nch:(t + 1) * nch], s_scr)
                if o_scr is not None:
                    for o, (d, off, lanes) in zip(outs, where[t * nch:(t + 1) * nch]):
                        o_scr[d, pl.ds(off, CHUNK), lanes] = o
            return carry

        lax.fori_loop(0, nc // REC_UNROLL, body, 0)

    def finish(o_scr, z_ref, out_ref):
        for hh in range(REC_HP):
            lanes = slice(hh * LANE, (hh + 1) * LANE)
            o = o_scr[0, :, lanes] + o_scr[1, :, lanes]
            z = z_ref[0, :, lanes]
            out_ref[0, :, lanes] = o * lax.rsqrt(_mean_sq(o) + EPS) * ng_ref[...] * (z * _sigmoid(z))

    run(qc_ref, kc_ref, vc_ref, gc_ref, oc)
    run(ql_ref, kl_ref, vl_ref, gl_ref, ol)
    finish(ol, zl_ref, y_ref)
    if need_ctx:
        finish(oc, zc_ref, yc_ref)


def _deltanet(p, pc, conv_w, a_log, dt_bias, norm_g, need_ctx):
    B, L, _ = p.shape
    Lc = pc.shape[1]
    H = DN_HEADS
    zero_b = jnp.zeros((1, 3 * GROUP_W), F32)
    u = _conv3(p, DN_Q, conv_w, zero_b, silu=True)
    uc = _conv3(pc, DN_Q, conv_w, zero_b, silu=True)

    def gates(src):
        base = SMALL * LANE + 4 * ML_HEADS
        beta = src[:, :, base:base + 2 * H].reshape(src.shape[0], src.shape[1], 2, 1, H)
        a = src[:, :, base + 2 * H:base + 4 * H].reshape(src.shape[0], src.shape[1], 2, 1, H)
        both = jnp.concatenate([a, beta], axis=3).reshape(src.shape[0], src.shape[1], 4 * H)
        return _gate_rows(both, 4, H)

    prm = jnp.stack([a_log[0], dt_bias[0], a_log[1], dt_bias[1]], axis=1).reshape(H, 4, 1, 1)
    W = REC_HP * LANE
    lat = lambda blk: pl.BlockSpec((1, L, W), lambda b, h: (b, 0, blk // REC_HP + h), pipeline_mode=pl.Buffered(1))
    ctx = lambda blk: pl.BlockSpec((1, Lc, W), lambda b, h: (b, 0, blk // REC_HP + h))
    gspec = lambda n: pl.BlockSpec((1, REC_HP, 4, n // CHUNK, CHUNK), lambda b, h: (b, h, 0, 0, 0))
    out_specs = [pl.BlockSpec((1, L, W), lambda b, h: (b, 0, h))]
    out_shape = [jax.ShapeDtypeStruct((B, L, GROUP_W), F32)]
    scratch = [pltpu.VMEM((DN_HEAD_DIM, DN_HEAD_DIM), F32)] * (2 * REC_HP) + [pltpu.VMEM((2, L, W), F32)]
    if need_ctx:
        out_specs.append(pl.BlockSpec((1, Lc, W), lambda b, h: (b, 0, h)))
        out_shape.append(jax.ShapeDtypeStruct((B, Lc, GROUP_W), F32))
        scratch.append(pltpu.VMEM((2, Lc, W), F32))
    res = pl.pallas_call(
        functools.partial(_deltanet_body, need_ctx=need_ctx),
        grid=(B, H // REC_HP),
        in_specs=[lat(0), lat(4), lat(8), lat(DN_G), ctx(0), ctx(4), ctx(8), ctx(DN_G),
                  gspec(L), gspec(Lc), pl.BlockSpec((REC_HP, 4, 1, 1), lambda b, h: (h, 0, 0, 0)),
                  pl.BlockSpec((1, LANE), lambda b, h: (0, 0))],
        out_specs=out_specs,
        out_shape=out_shape,
        scratch_shapes=scratch,
        compiler_params=_params(("arbitrary", "arbitrary"), 52),
        name="deltanet",
    )(u, u, u, p, uc, uc, uc, pc, gates(p), gates(pc), prm, norm_g[None, :])
    return (res[0], res[1]) if need_ctx else (res[0], None)


def _permute_w_in(w):
    D = w.shape[0]
    G = GROUP_W
    n_gate = 4 * ML_HEADS
    o = 0
    na, o = w[:, o:o + 3 * G], o + 3 * G
    ml_qkv, o = w[:, o:o + 3 * G], o + 3 * G
    ml_o, o = w[:, o:o + G], o + G
    ml_gates, o = w[:, o:o + n_gate], o + n_gate
    hy, o = w[:, o:o + 3 * G], o + 3 * G
    dn_qkv, o = w[:, o:o + 3 * G], o + 3 * G
    dn_gate, o = w[:, o:o + G], o + G
    dn_small = w[:, o:o + 4 * DN_HEADS]

    def split_pairs(m):
        return m.reshape(D, ML_HEADS, ML_HEAD_DIM // 2, 2).transpose(0, 1, 3, 2).reshape(D, G)

    used = 56 * LANE + n_gate + 4 * DN_HEADS
    parts = [na, split_pairs(ml_qkv[:, :G]), split_pairs(ml_qkv[:, G:2 * G]), ml_qkv[:, 2 * G:], ml_o, hy, dn_qkv,
             dn_gate, ml_gates, dn_small, jnp.zeros((D, P_COLS - used), w.dtype)]
    return jnp.concatenate(parts, axis=1).astype(BF16)


def kernel(x, c, ctx, c_ctx, w_ada, b_ada, norm_g, w_in, w_out, na_rpb, ml_gate_b, ml_norm_g, hy_conv_w, hy_conv_b, hy_f_w1, hy_f_b1, hy_f_w2, hy_f_b2, hy_f_w3, hy_f_b3, hy_skip, dn_conv_w, dn_a_log, dn_dt_bias, dn_norm_g, router_w, router_b, moe_w_gu, moe_b_gu, moe_w_down, moe_b_down):
    B, L, D = x.shape
    Lc = ctx.shape[1]
    assert D == D_MODEL and B < 16
    cc = jnp.zeros((16, D), F32).at[:B].set(c).at[B].set(c_ctx)
    mods = _adaln(cc, w_ada, b_ada)
    x2 = x.reshape(B * L, D)
    xc2 = ctx.reshape(B * Lc, D)
    flat = lambda y: y.reshape(-1, GROUP_W)
    rope = _rope_tables(L)
    dft_l = _dft_tables(L)
    dft_c = _dft_tables(Lc)
    for l in range(DEPTH):
        need_ctx = l < DEPTH - 1
        mod = mods[l, :B].reshape(B, 6, 1, D)
        mod_c = mods[l, B].reshape(1, 6, 1, D)
        g = norm_g[l]
        w_p = _permute_w_in(w_in[l])
        p = _inproj(x2, L, mod, g[0:1], w_p).reshape(B, L, P_COLS)
        pc = _inproj(xc2, Lc, mod_c, g[0:1], w_p).reshape(B, Lc, P_COLS)
        hy = (hy_conv_w[l], hy_conv_b[l], hy_f_w1[l], hy_f_b1[l], hy_f_w2[l], hy_f_b2[l], hy_f_w3[l], hy_f_b3[l],
              hy_skip[l])
        ya = _na_attention(p, pc, _na_bias_table(na_rpb[l]))
        yb, ybc = _mlstm(p, pc, rope, ml_gate_b[l], ml_norm_g[l], need_ctx)
        yh = _hyena(p, dft_l, *hy)
        yd, ydc = _deltanet(p, pc, dn_conv_w[l], dn_a_log[l], dn_dt_bias[l], dn_norm_g[l], need_ctx)
        w_out_bf = w_out[l].astype(BF16)
        rb = router_b[l][None, :]
        x2, tok, logits = _outproj([flat(ya), flat(yb), flat(yh), flat(yd)], w_out_bf, x2, L, mod, g[1:2], g[2:3],
                                   router_w[l], rb)
        if need_ctx:
            yac = _ctx_attention(pc)
            yhc = _hyena(pc, dft_c, *hy)
            xc2, tok_c, logits_c = _outproj([flat(yac), flat(ybc), flat(yhc), flat(ydc)], w_out_bf, xc2, Lc, mod_c,
                                            g[1:2], g[2:3], router_w[l], rb)
            tok = jnp.concatenate([tok, tok_c], axis=0)
            logits = jnp.concatenate([logits, logits_c], axis=0)
        dest3, gates, ys = _moe(tok, logits, moe_w_gu, moe_b_gu, moe_w_down, moe_b_down, l)
        nl = B * L // MOE_TM
        x2 = _combine(dest3[:nl], gates[:B * L], ys, x2, L, mod, g[3:4])
        if need_ctx:
            xc2 = _combine(dest3[nl:], gates[B * L:], ys, xc2, Lc, mod_c, g[3:4])
    return x2.reshape(B, L, D)
```
